```python
import math
import jax
import jax.numpy as jnp
from jax import lax
import numpy as np

D_MODEL = 1024
BATCH = 8
SEQ = 8192
DEPTH = 4

HEAD_DIM = 64
MIX_WIDTH = D_MODEL
A_HEADS = 4
DIFF_QK_DIM = HEAD_DIM // 2
DIFF_V_DIM = HEAD_DIM
B_HEADS = 6
C_HEADS = 6
DILATED_PATTERNS = ((128, 1), (512, 4), (2048, 16))
Q_BLOCK = 128
ROPE_THETA = 10000.0
D_FF = 2816
CONV_WIDTH = 3
NORM_EPS = 1e-6
SUBLN_EPS = 1e-5
FORGET_BIAS_INIT = 4.0

A_Q = A_HEADS * 2 * DIFF_QK_DIM
A_K = A_HEADS * 2 * DIFF_QK_DIM
A_V = A_HEADS * DIFF_V_DIM
B_W = B_HEADS * HEAD_DIM
C_W = C_HEADS * HEAD_DIM
IN_COLS = A_Q + A_K + A_V + 3 * B_W + 3 * C_W + C_HEADS

kernel_name = 'hybrid_diff_dilated_fox_block'


def rmsnorm(x, g, eps=NORM_EPS):
    xf = x.astype(jnp.float32)
    y = xf * lax.rsqrt(jnp.mean(xf * xf, axis=-1, keepdims=True) + eps)
    return (y * g.astype(jnp.float32)).astype(x.dtype)


def rope_tables(seq, dim):
    inv = 1.0 / (ROPE_THETA ** (jnp.arange(0, dim, 2, dtype=jnp.float32) / dim))
    ang = jnp.arange(seq, dtype=jnp.float32)[:, None] * inv[None, :]
    return jnp.cos(ang), jnp.sin(ang)


def apply_rope(x, cos, sin):
    half = x.shape[-1] // 2
    xf = x.astype(jnp.float32)
    x1, x2 = xf[..., :half], xf[..., half:]
    out = jnp.concatenate([x1 * cos - x2 * sin, x2 * cos + x1 * sin], axis=-1)
    return out.astype(x.dtype)


def diff_attention(q, k, v, lam):
    bn, h, _, t, dk = q.shape
    nb = t // Q_BLOCK
    qb = q.reshape(bn, h, 2, nb, Q_BLOCK, dk).transpose(3, 0, 1, 2, 4, 5)
    kpos = jnp.arange(t)
    scale = dk ** -0.5
    vf = v.astype(jnp.float32)

    def block(args):
        qblk, bi = args
        s = jnp.einsum('bhmqd,bhmkd->bhmqk', qblk, k).astype(jnp.float32) * scale
        qpos = bi * Q_BLOCK + jnp.arange(Q_BLOCK)
        s = jnp.where(kpos[None, :] <= qpos[:, None], s, -jnp.inf)
        p = jax.nn.softmax(s, axis=-1)
        a = p[:, :, 0] - lam * p[:, :, 1]
        return jnp.einsum('bhqk,bhkd->bhqd', a, vf)

    o = lax.map(block, (qb, jnp.arange(nb)))
    return o.transpose(1, 2, 0, 3, 4).reshape(bn, h, t, -1)


def forgetting_attention(q, k, v, cum_logf):
    bn, h, t, hd = q.shape
    nb = t // Q_BLOCK
    qb = q.reshape(bn, h, nb, Q_BLOCK, hd).transpose(2, 0, 1, 3, 4)
    fb = cum_logf.reshape(bn, h, nb, Q_BLOCK).transpose(2, 0, 1, 3)
    kpos = jnp.arange(t)
    scale = hd ** -0.5
    vf = v.astype(jnp.float32)

    def block(args):
        qblk, fq, bi = args
        s = jnp.einsum('bhqd,bhkd->bhqk', qblk, k).astype(jnp.float32) * scale
        s = s + fq[..., :, None] - cum_logf[..., None, :]
        qpos = bi * Q_BLOCK + jnp.arange(Q_BLOCK)
        s = jnp.where(kpos[None, :] <= qpos[:, None], s, -jnp.inf)
        p = jax.nn.softmax(s, axis=-1)
        return jnp.einsum('bhqk,bhkd->bhqd', p, vf)

    o = lax.map(block, (qb, fb, jnp.arange(nb)))
    return o.transpose(1, 2, 0, 3, 4).reshape(bn, h, t, hd)


def banded_window_attention(q, k, v, w):
    bn, g, l, hd = q.shape
    nb = l // w
    qb = q.reshape(bn, g, nb, w, hd)

    def with_prev(a):
        a = a.reshape(bn, g, nb, w, hd)
        prev = jnp.pad(a, ((0, 0), (0, 0), (1, 0), (0, 0), (0, 0)))[:, :, :-1]
        return jnp.concatenate([prev, a], axis=3)

    kk, vv = with_prev(k), with_prev(v)
    s = jnp.einsum('bgnqd,bgnkd->bgnqk', qb, kk).astype(jnp.float32) * (hd ** -0.5)
    dist = jnp.arange(w)[:, None] + w - jnp.arange(2 * w)[None, :]
    kidx = jnp.arange(nb)[:, None] * w + jnp.arange(2 * w)[None, :] - w
    mask = ((dist >= 0) & (dist <= w))[None, :, :] & (kidx >= 0)[:, None, :]
    s = jnp.where(mask, s, -jnp.inf)
    m = jnp.max(s, axis=-1, keepdims=True)
    p = jnp.exp(s - m)
    den = jnp.sum(p, axis=-1, keepdims=True)
    o = jnp.einsum('bgnqk,bgnkd->bgnqd', p, vv.astype(jnp.float32)) / den
    lse = (m + jnp.log(den))[..., 0]
    return o.reshape(bn, g, l, hd), lse.reshape(bn, g, l)


def dilated_mixture_attention(q, k, v):
    bn, h, t, hd = q.shape
    outs, lses = [], []
    for window, d in DILATED_PATTERNS:
        l = t // d
        w = window // d
        lp = -(-l // w) * w

        def fold(a):
            a = a.reshape(bn, h, l, d, hd).transpose(0, 1, 3, 2, 4).reshape(bn, h * d, l, hd)
            return jnp.pad(a, ((0, 0), (0, 0), (0, lp - l), (0, 0)))

        o, lse = banded_window_attention(fold(q), fold(k), fold(v), w)
        o = o[:, :, :l].reshape(bn, h, d, l, hd).transpose(0, 1, 3, 2, 4).reshape(bn, h, t, hd)
        lse = lse[:, :, :l].reshape(bn, h, d, l).transpose(0, 1, 3, 2).reshape(bn, h, t)
        outs.append(o)
        lses.append(lse)
    wts = jax.nn.softmax(jnp.stack(lses, axis=0), axis=0)
    return jnp.sum(wts[..., None] * jnp.stack(outs, axis=0), axis=0)


def causal_depthwise_conv(g, w, b):
    t = g.shape[1]
    gp = jnp.pad(g, ((0, 0), (CONV_WIDTH - 1, 0), (0, 0)))
    y = b
    for i in range(CONV_WIDTH):
        y = y + gp[:, i:i + t] * w[i]
    return y


def mixer_layer(h, w_in, w_out, lam_params, subln_g, forget_bias, lam_init, cos_a, sin_a, cos_b, sin_b):
    bn, t, _ = h.shape
    proj = h @ w_in
    cuts = np.cumsum([A_Q, A_K, A_V, B_W, B_W, B_W, C_W, C_W, C_W])
    qa, ka, va, qb, kb, vb, qc, kc, vc, fz = jnp.split(proj, cuts, axis=-1)

    qa = apply_rope(qa.reshape(bn, t, A_HEADS, 2, DIFF_QK_DIM).transpose(0, 2, 3, 1, 4), cos_a, sin_a)
    ka = apply_rope(ka.reshape(bn, t, A_HEADS, 2, DIFF_QK_DIM).transpose(0, 2, 3, 1, 4), cos_a, sin_a)
    va = va.reshape(bn, t, A_HEADS, DIFF_V_DIM).transpose(0, 2, 1, 3)
    lp = lam_params.astype(jnp.float32)
    lam = jnp.exp(jnp.sum(lp[0] * lp[1])) - jnp.exp(jnp.sum(lp[2] * lp[3])) + lam_init
    oa = diff_attention(qa, ka, va, lam)
    oa = oa * lax.rsqrt(jnp.mean(oa * oa, axis=-1, keepdims=True) + SUBLN_EPS)
    oa = oa * subln_g.astype(jnp.float32) * (1.0 - lam_init)

    def heads(a, n):
        return a.reshape(bn, t, n, HEAD_DIM).transpose(0, 2, 1, 3)
    qb = apply_rope(heads(qb, B_HEADS), cos_b, sin_b)
    kb = apply_rope(heads(kb, B_HEADS), cos_b, sin_b)
    ob = dilated_mixture_attention(qb, kb, heads(vb, B_HEADS))

    logf = jax.nn.log_sigmoid((fz + forget_bias).astype(jnp.float32))
    cum_logf = lax.cumsum(logf, axis=1).transpose(0, 2, 1)
    oc = forgetting_attention(heads(qc, C_HEADS), heads(kc, C_HEADS), heads(vc, C_HEADS), cum_logf)

    def merge(o):
        return o.transpose(0, 2, 1, 3).reshape(bn, t, -1)
    o = jnp.concatenate([merge(oa), merge(ob), merge(oc)], axis=-1).astype(h.dtype)
    return o @ w_out


def setup_inputs(seed: int = 0) -> dict:
    key = jax.random.key(seed)
    ks = jax.random.split(key, 18)
    f32 = jnp.float32
    n = jax.random.normal
    return {
        'x': n(ks[0], (BATCH, SEQ, D_MODEL), f32),
        'c': n(ks[1], (BATCH, D_MODEL), f32),
        'w_mod': n(ks[2], (DEPTH, D_MODEL, 6 * D_MODEL), f32) * (0.5 * D_MODEL ** -0.5),
        'b_mod': n(ks[3], (DEPTH, 6 * D_MODEL), f32) * 0.02,
        'g_attn': 1.0 + 0.02 * n(ks[4], (DEPTH, D_MODEL), f32),
        'w_in': n(ks[5], (DEPTH, D_MODEL, IN_COLS), f32) * D_MODEL ** -0.5,
        'diff_lambda': n(ks[6], (DEPTH, 4, DIFF_QK_DIM), f32) * 0.1,
        'subln_g': 1.0 + 0.02 * n(ks[7], (DEPTH, DIFF_V_DIM), f32),
        'forget_bias': FORGET_BIAS_INIT + 0.5 * n(ks[8], (DEPTH, C_HEADS), f32),
        'w_out': n(ks[9], (DEPTH, MIX_WIDTH, D_MODEL), f32) * MIX_WIDTH ** -0.5,
        'g_mlp': 1.0 + 0.02 * n(ks[10], (DEPTH, D_MODEL), f32),
        'w_up': n(ks[11], (DEPTH, D_MODEL, 2 * D_FF), f32) * D_MODEL ** -0.5,
        'conv_w': n(ks[12], (DEPTH, CONV_WIDTH, D_FF), f32) * CONV_WIDTH ** -0.5,
        'conv_b': n(ks[13], (DEPTH, D_FF), f32) * 0.02,
        'w_down': n(ks[14], (DEPTH, D_FF, D_MODEL), f32) * D_FF ** -0.5,
        'g_final': 1.0 + 0.02 * n(ks[15], (D_MODEL,), f32),
    }


def reference(x, c, w_mod, b_mod, g_attn, w_in, diff_lambda, subln_g, forget_bias, w_out,
              g_mlp, w_up, conv_w, conv_b, w_down, g_final):
    t = x.shape[1]
    cos_a, sin_a = rope_tables(t, DIFF_QK_DIM)
    cos_b, sin_b = rope_tables(t, HEAD_DIM)
    sc = jax.nn.silu(c)
    for layer in range(DEPTH):
        lam_init = 0.8 - 0.6 * math.exp(-0.3 * layer)
        mod = sc @ w_mod[layer] + b_mod[layer]
        shift1, scale1, gate1, shift2, scale2, gate2 = [m[:, None, :] for m in jnp.split(mod, 6, axis=-1)]

        h = rmsnorm(x, g_attn[layer]) * (1.0 + scale1) + shift1
        o = mixer_layer(h, w_in[layer], w_out[layer], diff_lambda[layer], subln_g[layer],
                        forget_bias[layer], lam_init, cos_a, sin_a, cos_b, sin_b)
        x = x + gate1 * o

        h = rmsnorm(x, g_mlp[layer]) * (1.0 + scale2) + shift2
        up = h @ w_up[layer]
        u, g = up[..., :D_FF], up[..., D_FF:]
        g = causal_depthwise_conv(g, conv_w[layer], conv_b[layer])
        x = x + gate2 * ((jax.nn.silu(g) * u) @ w_down[layer])
    return rmsnorm(x, g_final)
```

```python
import functools
import math

import jax
import jax.numpy as jnp
from jax import lax
from jax.experimental import pallas as pl
from jax.experimental.pallas import tpu as pltpu

F32 = jnp.float32
BF16 = jnp.bfloat16

D_MODEL = 1024
HEAD_DIM = 64
A_HEADS = 4
DIFF_QK_DIM = 32
B_HEADS = 6
C_HEADS = 6
DILATED_PATTERNS = ((128, 1), (512, 4), (2048, 16))
ROPE_THETA = 10000.0
D_FF = 2816
CONV_WIDTH = 3
NORM_EPS = 1e-6
SUBLN_EPS = 1e-5

A_W = A_HEADS * HEAD_DIM
B_W = B_HEADS * HEAD_DIM
C_W = C_HEADS * HEAD_DIM
IN_COLS = 3 * A_W + 3 * B_W + 3 * C_W + C_HEADS
LANES = 128
IN_COLS_PAD = 3 * A_W + 3 * B_W + 3 * C_W + LANES

LOG2E = 1.4426950408889634
NEG = -1e30
VMEM_LIMIT = 56 * 2**20

TM = 512
TQ = 512
DIL_TILE = 2048
BAND = 128


def _cparams(*sem):
    return pltpu.CompilerParams(dimension_semantics=sem, vmem_limit_bytes=VMEM_LIMIT)


def _nt_dot(a, b):
    return lax.dot_general(a, b, (((1,), (1,)), ((), ())), preferred_element_type=F32)


def _mod_kernel(c_ref, w_ref, b_ref, o_ref):
    c = c_ref[...]
    sc = c * (1.0 / (1.0 + jnp.exp(-c)))
    o_ref[0] = jnp.dot(sc, w_ref[0], preferred_element_type=F32,
                       precision=lax.Precision.HIGHEST) + b_ref[0]


def _modulation(c, w_mod, b_mod):
    depth, d, n = w_mod.shape
    bn = c.shape[0]
    nb = n // d
    return pl.pallas_call(
        _mod_kernel,
        grid=(depth, nb),
        in_specs=[
            pl.BlockSpec((bn, d), lambda l, j: (0, 0)),
            pl.BlockSpec((1, d, d), lambda l, j: (l, 0, j)),
            pl.BlockSpec((1, 1, d), lambda l, j: (l, 0, j)),
        ],
        out_specs=pl.BlockSpec((1, bn, d), lambda l, j: (l, 0, j)),
        out_shape=jax.ShapeDtypeStruct((depth, bn, n), F32),
        compiler_params=_cparams("parallel", "parallel"),
        name="modulation",
    )(c, w_mod, b_mod.reshape(depth, 1, n))


def _modulated_norm(x, g, scale, shift):
    ms = jnp.mean(x * x, axis=-1, keepdims=True)
    y = x * lax.rsqrt(ms + NORM_EPS) * g
    return y * (1.0 + scale) + shift


def _rope128(r, cos, sin_signed, group):
    half = group // 2
    lane = lax.broadcasted_iota(jnp.int32, r.shape, 1)
    first = (lane % group) < half
    rot = jnp.where(first, pltpu.roll(r, LANES - half, 1), pltpu.roll(r, half, 1))
    return r * cos + rot * sin_signed


def _inproj_kernel(x_ref, g_ref, sc_ref, sh_ref, w_ref, cosa_ref, sina_ref, cosb_ref, sinb_ref,
                   qa_ref, ka_ref, va_ref, qb_ref, kb_ref, vb_ref, qc_ref, kc_ref, vc_ref, fz_ref):
    h = _modulated_norm(x_ref[0], g_ref[...], sc_ref[0], sh_ref[0]).astype(BF16)
    cosa, sina = cosa_ref[...], sina_ref[...]
    cosb, sinb = cosb_ref[...], sinb_ref[...]
    qa_scale = DIFF_QK_DIM ** -0.5 * LOG2E
    qb_scale = HEAD_DIM ** -0.5 * LOG2E
    qc_scale = HEAD_DIM ** -0.5 * LOG2E

    ra = jnp.dot(h, w_ref[:, 0:3 * A_W], preferred_element_type=F32)
    for blk in range(A_W // LANES):
        sl = slice(blk * LANES, (blk + 1) * LANES)
        q = ra[:, blk * LANES:(blk + 1) * LANES]
        k = ra[:, A_W + blk * LANES:A_W + (blk + 1) * LANES]
        qa_ref[0, :, sl] = (_rope128(q, cosa, sina, DIFF_QK_DIM) * qa_scale).astype(BF16)
        ka_ref[0, :, sl] = _rope128(k, cosa, sina, DIFF_QK_DIM).astype(BF16)
    va_ref[0] = ra[:, 2 * A_W:3 * A_W].astype(BF16)

    off = 3 * A_W
    rb = jnp.dot(h, w_ref[:, off:off + 3 * B_W], preferred_element_type=F32)
    for blk in range(B_W // LANES):
        sl = slice(blk * LANES, (blk + 1) * LANES)
        q = rb[:, blk * LANES:(blk + 1) * LANES]
        k = rb[:, B_W + blk * LANES:B_W + (blk + 1) * LANES]
        qb_ref[0, :, sl] = _rope128(q, cosb, sinb, HEAD_DIM) * qb_scale
        kb_ref[0, :, sl] = _rope128(k, cosb, sinb, HEAD_DIM)
    vb_ref[0] = rb[:, 2 * B_W:3 * B_W]

    off = 3 * A_W + 3 * B_W
    rc = jnp.dot(h, w_ref[:, off:off + 3 * C_W + LANES], preferred_element_type=F32)
    qc_ref[0] = (rc[:, 0:C_W] * qc_scale).astype(BF16)
    kc_ref[0] = rc[:, C_W:2 * C_W].astype(BF16)
    vc_ref[0] = rc[:, 2 * C_W:3 * C_W].astype(BF16)
    fz_ref[0] = rc[:, 3 * C_W:3 * C_W + LANES]


def _in_projection(x, g, scale, shift, w_in_p, tabs):
    bn, t, d = x.shape
    cosa, sina, cosb, sinb = tabs
    tok = lambda w: pl.BlockSpec((1, TM, w), lambda b, i: (b, i, 0))
    per_b = pl.BlockSpec((1, 1, d), lambda b, i: (b, 0, 0))
    tab = pl.BlockSpec((TM, LANES), lambda b, i: (i, 0))
    widths = [A_W] * 3 + [B_W] * 3 + [C_W] * 3 + [LANES]
    dtypes = [BF16] * 3 + [F32] * 3 + [BF16] * 3 + [F32]
    return pl.pallas_call(
        _inproj_kernel,
        grid=(bn, t // TM),
        in_specs=[tok(d), pl.BlockSpec((1, d), lambda b, i: (0, 0)), per_b, per_b,
                  pl.BlockSpec((d, IN_COLS_PAD), lambda b, i: (0, 0)), tab, tab, tab, tab],
        out_specs=[tok(w) for w in widths],
        out_shape=[jax.ShapeDtypeStruct((bn, t, w), dt) for w, dt in zip(widths, dtypes)],
        compiler_params=_cparams("parallel", "parallel"),
        name="in_projection",
    )(x, g.reshape(1, d), scale, shift, w_in_p, cosa, sina, cosb, sinb)


def _cumlogf_kernel(z_ref, b_ref, u_ref, lb_ref, o_ref):
    z = z_ref[0] + b_ref[...]
    logf = jnp.minimum(z, 0.0) - jnp.log1p(jnp.exp(-jnp.abs(z)))
    hi = lax.Precision.HIGHEST
    within = jnp.dot(logf, u_ref[...], preferred_element_type=F32, precision=hi)
    before = jnp.dot(lb_ref[...], within, preferred_element_type=F32, precision=hi)
    o_ref[0] = (within + before[:, LANES - 1:LANES]) * LOG2E


def _cum_logf(fz, forget_bias):
    bn, t, _ = fz.shape
    nc = t // LANES
    rows = 8 * nc
    z = jnp.transpose(fz[:, :, :8], (0, 2, 1)).reshape(bn, rows, LANES)
    bias = jnp.concatenate([forget_bias.astype(F32), jnp.zeros((8 - C_HEADS,), F32)])
    bias = jnp.broadcast_to(jnp.repeat(bias, nc)[:, None], (rows, LANES))
    idx = jnp.arange(LANES)
    upper = (idx[:, None] <= idx[None, :]).astype(F32)
    r = jnp.arange(rows)
    lblk = ((r[:, None] // nc == r[None, :] // nc) & (r[None, :] < r[:, None])).astype(F32)
    cum = pl.pallas_call(
        _cumlogf_kernel,
        grid=(bn,),
        in_specs=[pl.BlockSpec((1, rows, LANES), lambda b: (b, 0, 0)),
                  pl.BlockSpec((rows, LANES), lambda b: (0, 0)),
                  pl.BlockSpec((LANES, LANES), lambda b: (0, 0)),
                  pl.BlockSpec((rows, rows), lambda b: (0, 0))],
        out_specs=pl.BlockSpec((1, rows, LANES), lambda b: (b, 0, 0)),
        out_shape=jax.ShapeDtypeStruct((bn, rows, LANES), F32),
        compiler_params=_cparams("parallel"),
        name="cum_logf",
    )(z, bias, upper, lblk)
    return cum.reshape(bn, 8, t)


def _online_update(s_blocks, v2, m_sc, l_sc, acc_sc, idx):
    mpart = s_blocks[0]
    for sb in s_blocks[1:]:
        mpart = jnp.maximum(mpart, sb)
    m_prev = m_sc[idx]
    m_new = jnp.maximum(m_prev, jnp.max(mpart, axis=-1, keepdims=True))
    alpha = jnp.exp2(m_prev - m_new)
    ps = [jnp.exp2(sb - m_new) for sb in s_blocks]
    lsum = ps[0]
    for p in ps[1:]:
        lsum = lsum + p
    p = jnp.concatenate([p.astype(BF16) for p in ps], axis=1)
    pv = jnp.dot(p, v2, preferred_element_type=F32)
    l_sc[idx] = alpha * l_sc[idx] + lsum
    acc_sc[idx] = alpha * acc_sc[idx] + pv
    m_sc[idx] = m_new


def _causal_mask_blocks(s_blocks, tq):
    row = lax.broadcasted_iota(jnp.int32, (tq, LANES), 0)
    lane = lax.broadcasted_iota(jnp.int32, (tq, LANES), 1)
    return [jnp.where(lane + cb * LANES <= row, sb, NEG) for cb, sb in enumerate(s_blocks)]


def _lane_blocks(s):
    return [s[:, cb * LANES:(cb + 1) * LANES] for cb in range(s.shape[1] // LANES)]


def _fox_kernel(q_ref, k_ref, v_ref, f_ref, c_ref, o_ref, m_sc, l_sc, acc_sc, *, tq):
    i = pl.program_id(2)
    q2 = q_ref[0]
    lane = lax.broadcasted_iota(jnp.int32, (tq, LANES), 1)
    zero = jnp.zeros_like(q2)
    qh = (jnp.where(lane < HEAD_DIM, q2, zero), jnp.where(lane >= HEAD_DIM, q2, zero))
    f = f_ref[0, 0]
    frep = [jnp.broadcast_to(f[:, h:h + 1], (tq, LANES)) for h in range(2)]
    m_sc[...] = jnp.full(m_sc.shape, NEG, F32)
    l_sc[...] = jnp.zeros(l_sc.shape, F32)
    acc_sc[...] = jnp.zeros(acc_sc.shape, F32)

    def step(j, masked):
        start = pl.multiple_of(j * tq, tq)
        k2 = k_ref[0, pl.ds(start, tq), :]
        v2 = v_ref[0, pl.ds(start, tq), :]
        cj = c_ref[0, 0, j]
        for h in range(2):
            blocks = _lane_blocks(_nt_dot(qh[h], k2))
            blocks = [sb + (frep[h] - cj[h:h + 1, cb * LANES:(cb + 1) * LANES])
                      for cb, sb in enumerate(blocks)]
            if masked:
                blocks = _causal_mask_blocks(blocks, tq)
            _online_update(blocks, v2, m_sc, l_sc, acc_sc, h)

    def body(j, carry):
        step(j, False)
        return carry

    lax.fori_loop(0, i, body, 0)
    step(i, True)

    outs = []
    for h in range(2):
        l = jnp.sum(l_sc[h], axis=-1, keepdims=True)
        outs.append(acc_sc[h] / l)
    o_ref[0] = jnp.where(lane < HEAD_DIM, outs[0], outs[1]).astype(o_ref.dtype)


def _fox_attention(qc, kc, vc, cum):
    bn, t, _ = qc.shape
    tq = TQ
    nq = t // tq
    hp = C_HEADS // 2
    c6 = cum[:, :C_HEADS].reshape(bn, hp, 2, t)
    fcol = jnp.transpose(c6, (0, 1, 3, 2))
    crow = jnp.transpose(c6.reshape(bn, hp, 2, nq, tq), (0, 1, 3, 2, 4))
    return pl.pallas_call(
        functools.partial(_fox_kernel, tq=tq),
        grid=(bn, hp, nq),
        in_specs=[pl.BlockSpec((1, tq, LANES), lambda b, p, i: (b, i, p)),
                  pl.BlockSpec((1, t, LANES), lambda b, p, i: (b, 0, p)),
                  pl.BlockSpec((1, t, LANES), lambda b, p, i: (b, 0, p)),
                  pl.BlockSpec((1, 1, tq, 2), lambda b, p, i: (b, p, i, 0)),
                  pl.BlockSpec((1, 1, nq, 2, tq), lambda b, p, i: (b, p, 0, 0, 0))],
        out_specs=pl.BlockSpec((1, tq, LANES), lambda b, p, i: (b, i, p)),
        out_shape=jax.ShapeDtypeStruct((bn, t, C_W), BF16),
        scratch_shapes=[pltpu.VMEM((2, tq, LANES), F32)] * 3,
        compiler_params=_cparams("parallel", "parallel", "arbitrary"),
        name="forgetting_attention",
    )(qc, kc, vc, fcol, crow)


def _diff_kernel(q_ref, k_ref, v_ref, lp_ref, g_ref, o_ref, m_sc, l_sc, acc_sc, *, tq, lam_init):
    i = pl.program_id(2)
    q2 = q_ref[0]
    lane = lax.broadcasted_iota(jnp.int32, (tq, LANES), 1)
    zero = jnp.zeros_like(q2)
    nmap = LANES // DIFF_QK_DIM
    qm = [jnp.where((lane >= g * DIFF_QK_DIM) & (lane < (g + 1) * DIFF_QK_DIM), q2, zero)
          for g in range(nmap)]
    m_sc[...] = jnp.full(m_sc.shape, NEG, F32)
    l_sc[...] = jnp.zeros(l_sc.shape, F32)
    acc_sc[...] = jnp.zeros(acc_sc.shape, F32)

    def step(j, masked):
        start = pl.multiple_of(j * tq, tq)
        k2 = k_ref[0, pl.ds(start, tq), :]
        v2 = v_ref[0, pl.ds(start, tq), :]
        for g in range(nmap):
            blocks = _lane_blocks(_nt_dot(qm[g], k2))
            if masked:
                blocks = _causal_mask_blocks(blocks, tq)
            _online_update(blocks, v2, m_sc, l_sc, acc_sc, g)

    def body(j, carry):
        step(j, False)
        return carry

    lax.fori_loop(0, i, body, 0)
    step(i, True)

    lp = lp_ref[...]
    lam = (jnp.exp(jnp.sum(lp[0:1] * lp[1:2], axis=-1, keepdims=True))
           - jnp.exp(jnp.sum(lp[2:3] * lp[3:4], axis=-1, keepdims=True)) + lam_init)
    norm = [acc_sc[g] / jnp.sum(l_sc[g], axis=-1, keepdims=True) for g in range(nmap)]
    first = lane < HEAD_DIM
    o = jnp.where(first, norm[0] - lam * norm[1], norm[2] - lam * norm[3])
    osq = o * o
    ms0 = jnp.sum(jnp.where(first, osq, 0.0), axis=-1, keepdims=True) * (1.0 / HEAD_DIM)
    ms1 = jnp.sum(jnp.where(first, 0.0, osq), axis=-1, keepdims=True) * (1.0 / HEAD_DIM)
    o = o * lax.rsqrt(jnp.where(first, ms0, ms1) + SUBLN_EPS)
    o_ref[0] = (o * g_ref[...] * (1.0 - lam_init)).astype(o_ref.dtype)


def _diff_attention(qa, ka, va, lam_params, subln_g, lam_init):
    bn, t, _ = qa.shape
    tq = TQ
    nq = t // tq
    hp = A_HEADS // 2
    g2 = jnp.concatenate([subln_g, subln_g]).reshape(1, LANES).astype(F32)
    return pl.pallas_call(
        functools.partial(_diff_kernel, tq=tq, lam_init=lam_init),
        grid=(bn, hp, nq),
        in_specs=[pl.BlockSpec((1, tq, LANES), lambda b, p, i: (b, i, p)),
                  pl.BlockSpec((1, t, LANES), lambda b, p, i: (b, 0, p)),
                  pl.BlockSpec((1, t, LANES), lambda b, p, i: (b, 0, p)),
                  pl.BlockSpec((4, DIFF_QK_DIM), lambda b, p, i: (0, 0)),
                  pl.BlockSpec((1, LANES), lambda b, p, i: (0, 0))],
        out_specs=pl.BlockSpec((1, tq, LANES), lambda b, p, i: (b, i, p)),
        out_shape=jax.ShapeDtypeStruct((bn, t, A_W), BF16),
        scratch_shapes=[pltpu.VMEM((4, tq, LANES), F32)] * 3,
        compiler_params=_cparams("parallel", "parallel", "arbitrary"),
        name="diff_attention",
    )(qa, ka, va, lam_params.astype(F32), g2)


def _dilated_kernel(q_ref, k_ref, v_ref, o_ref, m_sc, l_sc, acc_sc, *, tile):
    base = pl.program_id(2) * tile
    row = lax.broadcasted_iota(jnp.int32, (BAND, LANES), 0)
    lane = lax.broadcasted_iota(jnp.int32, (BAND, LANES), 1)
    bias_prev = jnp.where(lane >= row, 0.0, NEG)
    bias_cur = jnp.where(lane <= row, 0.0, NEG)
    first = lane < HEAD_DIM
    m_sc[...] = jnp.full(m_sc.shape, NEG, F32)
    l_sc[...] = jnp.zeros(l_sc.shape, F32)
    acc_sc[...] = jnp.zeros(acc_sc.shape, F32)

    for window, d in DILATED_PATTERNS:
        span = BAND * d
        assert window == span and tile % span == 0

        def rows(start, d=d):
            return pl.ds(start, BAND) if d == 1 else pl.ds(start, BAND, stride=d)

        def body(idx, carry, d=d, span=span, rows=rows):
            u = idx // d
            r = idx - u * d
            qs = u * span + r
            cur = base + qs
            prev = cur - span
            has_prev = prev >= 0
            prev = jnp.where(has_prev, prev, cur)
            q = q_ref[0, rows(qs), :].astype(BF16)
            k_prev = k_ref[0, rows(prev), :].astype(BF16)
            k_cur = k_ref[0, rows(cur), :].astype(BF16)
            vcat = jnp.concatenate([v_ref[0, rows(prev), :], v_ref[0, rows(cur), :]], axis=0).astype(BF16)
            bprev = jnp.where(has_prev, bias_prev, NEG)
            zero = jnp.zeros_like(q)
            for h in range(2):
                qh = jnp.where(first, q, zero) if h == 0 else jnp.where(first, zero, q)
                s_prev = _nt_dot(qh, k_prev) + bprev
                s_cur = _nt_dot(qh, k_cur) + bias_cur
                m_prev = m_sc[h, rows(qs), :]
                m_new = jnp.maximum(m_prev, jnp.max(jnp.maximum(s_prev, s_cur), axis=-1, keepdims=True))
                alpha = jnp.exp2(m_prev - m_new)
                p_prev = jnp.exp2(s_prev - m_new)
                p_cur = jnp.exp2(s_cur - m_new)
                p = jnp.concatenate([p_prev.astype(BF16), p_cur.astype(BF16)], axis=1)
                pv = jnp.dot(p, vcat, preferred_element_type=F32)
                l_sc[h, rows(qs), :] = alpha * l_sc[h, rows(qs), :] + (p_prev + p_cur)
                acc_sc[h, rows(qs), :] = alpha * acc_sc[h, rows(qs), :] + pv
                m_sc[h, rows(qs), :] = m_new
            return carry

        lax.fori_loop(0, tile // BAND, body, 0)

    lane_t = lax.broadcasted_iota(jnp.int32, (tile, LANES), 1)
    outs = [acc_sc[h] / jnp.sum(l_sc[h], axis=-1, keepdims=True) for h in range(2)]
    o_ref[0] = jnp.where(lane_t < HEAD_DIM, outs[0], outs[1]).astype(o_ref.dtype)


def _dilated_attention(qb, kb, vb):
    bn, t, _ = qb.shape
    tile = DIL_TILE
    hp = B_HEADS // 2
    return pl.pallas_call(
        functools.partial(_dilated_kernel, tile=tile),
        grid=(bn, hp, t // tile),
        in_specs=[pl.BlockSpec((1, tile, LANES), lambda b, p, i: (b, i, p)),
                  pl.BlockSpec((1, t, LANES), lambda b, p, i: (b, 0, p)),
                  pl.BlockSpec((1, t, LANES), lambda b, p, i: (b, 0, p))],
        out_specs=pl.BlockSpec((1, tile, LANES), lambda b, p, i: (b, i, p)),
        out_shape=jax.ShapeDtypeStruct((bn, t, B_W), BF16),
        scratch_shapes=[pltpu.VMEM((2, tile, LANES), F32)] * 3,
        compiler_params=_cparams("parallel", "parallel", "arbitrary"),
        name="dilated_attention",
    )(qb, kb, vb)


def _outproj_kernel(oa_ref, ob_ref, oc_ref, x_ref, w_ref, gate_ref, g_ref, sc_ref, sh_ref, x1_ref, h2_ref):
    o = jnp.concatenate([oa_ref[0], ob_ref[0], oc_ref[0]], axis=-1)
    x1 = x_ref[0] + gate_ref[0] * jnp.dot(o, w_ref[...], preferred_element_type=F32)
    x1_ref[0] = x1
    h2_ref[0] = _modulated_norm(x1, g_ref[...], sc_ref[0], sh_ref[0]).astype(BF16)


def _out_projection(oa, ob, oc, x, w_out, gate1, g_mlp, scale2, shift2):
    bn, t, d = x.shape
    tok = lambda w: pl.BlockSpec((1, TM, w), lambda b, i: (b, i, 0))
    per_b = pl.BlockSpec((1, 1, d), lambda b, i: (b, 0, 0))
    return pl.pallas_call(
        _outproj_kernel,
        grid=(bn, t // TM),
        in_specs=[tok(A_W), tok(B_W), tok(C_W), tok(d),
                  pl.BlockSpec((d, d), lambda b, i: (0, 0)), per_b,
                  pl.BlockSpec((1, d), lambda b, i: (0, 0)), per_b, per_b],
        out_specs=[tok(d), tok(d)],
        out_shape=[jax.ShapeDtypeStruct((bn, t, d), F32), jax.ShapeDtypeStruct((bn, t, d), BF16)],
        compiler_params=_cparams("parallel", "parallel"),
        name="out_projection",
    )(oa, ob, oc, x, w_out, gate1, g_mlp.reshape(1, d), scale2, shift2)


FF_CHUNK = 256
HALO = 8


def _up_kernel(h_ref, w_ref, cw_ref, cb_ref, y_ref, halo_sc, g_sc):
    i = pl.program_id(1)

    @pl.when(i == 0)
    def _():
        halo_sc[...] = jnp.zeros(halo_sc.shape, F32)

    h = h_ref[0]
    tm = h.shape[0]
    for c in range(D_FF // FF_CHUNK):
        cs = slice(c * FF_CHUNK, (c + 1) * FF_CHUNK)
        u = jnp.dot(h, w_ref[:, cs], preferred_element_type=F32)
        g = jnp.dot(h, w_ref[:, D_FF + c * FF_CHUNK:D_FF + (c + 1) * FF_CHUNK], preferred_element_type=F32)
        g_sc[0:HALO, :] = halo_sc[:, cs]
        g_sc[HALO:HALO + tm, :] = g
        halo_sc[:, cs] = g[tm - HALO:tm, :]
        conv = (cb_ref[:, cs] + g_sc[HALO - 2:HALO - 2 + tm, :] * cw_ref[0:1, cs]
                + g_sc[HALO - 1:HALO - 1 + tm, :] * cw_ref[1:2, cs] + g * cw_ref[2:3, cs])
        act = conv * (1.0 / (1.0 + jnp.exp(-conv)))
        y_ref[0, :, cs] = (act * u).astype(BF16)


def _up_projection(h2, w_up, conv_w, conv_b):
    bn, t, d = h2.shape
    return pl.pallas_call(
        _up_kernel,
        grid=(bn, t // TM),
        in_specs=[pl.BlockSpec((1, TM, d), lambda b, i: (b, i, 0)),
                  pl.BlockSpec((d, 2 * D_FF), lambda b, i: (0, 0)),
                  pl.BlockSpec((CONV_WIDTH, D_FF), lambda b, i: (0, 0)),
                  pl.BlockSpec((1, D_FF), lambda b, i: (0, 0))],
        out_specs=pl.BlockSpec((1, TM, D_FF), lambda b, i: (b, i, 0)),
        out_shape=jax.ShapeDtypeStruct((bn, t, D_FF), BF16),
        scratch_shapes=[pltpu.VMEM((HALO, D_FF), F32), pltpu.VMEM((HALO + TM, FF_CHUNK), F32)],
        compiler_params=_cparams("parallel", "arbitrary"),
        name="up_projection",
    )(h2, w_up, conv_w, conv_b.reshape(1, D_FF))


def _down_kernel(y_ref, x_ref, w_ref, gate_ref, g_ref, o_ref, *, final_norm):
    x2 = x_ref[0] + gate_ref[0] * jnp.dot(y_ref[0], w_ref[...], preferred_element_type=F32)
    if final_norm:
        ms = jnp.mean(x2 * x2, axis=-1, keepdims=True)
        x2 = x2 * lax.rsqrt(ms + NORM_EPS) * g_ref[...]
    o_ref[0] = x2


def _down_projection(y, x1, w_down, gate2, g_final, final_norm):
    bn, t, d = x1.shape
    return pl.pallas_call(
        functools.partial(_down_kernel, final_norm=final_norm),
        grid=(bn, t // TM),
        in_specs=[pl.BlockSpec((1, TM, D_FF), lambda b, i: (b, i, 0)),
                  pl.BlockSpec((1, TM, d), lambda b, i: (b, i, 0)),
                  pl.BlockSpec((D_FF, d), lambda b, i: (0, 0)),
                  pl.BlockSpec((1, 1, d), lambda b, i: (b, 0, 0)),
                  pl.BlockSpec((1, d), lambda b, i: (0, 0))],
        out_specs=pl.BlockSpec((1, TM, d), lambda b, i: (b, i, 0)),
        out_shape=jax.ShapeDtypeStruct((bn, t, d), F32),
        compiler_params=_cparams("parallel", "parallel"),
        name="down_projection",
    )(y, x1, w_down, gate2, g_final.reshape(1, d))


def _rope_table(t, dim):
    inv = 1.0 / (ROPE_THETA ** (jnp.arange(0, dim, 2, dtype=F32) / dim))
    ang = jnp.arange(t, dtype=F32)[:, None] * inv[None, :]
    cos, sin = jnp.cos(ang), jnp.sin(ang)
    reps = LANES // dim
    return (jnp.tile(jnp.concatenate([cos, cos], axis=-1), (1, reps)),
            jnp.tile(jnp.concatenate([-sin, sin], axis=-1), (1, reps)))


def kernel(x, c, w_mod, b_mod, g_attn, w_in, diff_lambda, subln_g, forget_bias, w_out,
           g_mlp, w_up, conv_w, conv_b, w_down, g_final):
    depth = w_mod.shape[0]
    bn, t, d = x.shape
    assert d == D_MODEL and t % DIL_TILE == 0 and t % TM == 0 and t % TQ == 0
    tabs = _rope_table(t, DIFF_QK_DIM) + _rope_table(t, HEAD_DIM)
    mod = _modulation(c, w_mod, b_mod)
    w_in_p = jnp.pad(w_in, ((0, 0), (0, 0), (0, IN_COLS_PAD - IN_COLS))).astype(BF16)
    w_out_b, w_up_b, w_down_b = w_out.astype(BF16), w_up.astype(BF16), w_down.astype(BF16)

    for layer in range(depth):
        lam_init = 0.8 - 0.6 * math.exp(-0.3 * layer)
        shift1, scale1, gate1, shift2, scale2, gate2 = [
            m[:, None, :] for m in jnp.split(mod[layer], 6, axis=-1)]
        qa, ka, va, qb, kb, vb, qc, kc, vc, fz = _in_projection(
            x, g_attn[layer], scale1, shift1, w_in_p[layer], tabs)
        cum = _cum_logf(fz, forget_bias[layer])
        oa = _diff_attention(qa, ka, va, diff_lambda[layer], subln_g[layer], lam_init)
        ob = _dilated_attention(qb, kb, vb)
        oc = _fox_attention(qc, kc, vc, cum)
        x1, h2 = _out_projection(oa, ob, oc, x, w_out_b[layer], gate1, g_mlp[layer], scale2, shift2)
        y = _up_projection(h2, w_up_b[layer], conv_w[layer], conv_b[layer])
        x = _down_projection(y, x1, w_down_b[layer], gate2, g_final, layer == depth - 1)
    return x
```

```python
import functools
import math

import jax
import jax.numpy as jnp
from jax import lax
from jax.experimental import pallas as pl
from jax.experimental.pallas import tpu as pltpu

F32 = jnp.float32
BF16 = jnp.bfloat16

D_MODEL = 1024
HEAD_DIM = 64
A_HEADS = 4
DIFF_QK_DIM = 32
B_HEADS = 6
C_HEADS = 6
DILATED_PATTERNS = ((128, 1), (512, 4), (2048, 16))
ROPE_THETA = 10000.0
D_FF = 2816
CONV_WIDTH = 3
NORM_EPS = 1e-6
SUBLN_EPS = 1e-5

A_W = A_HEADS * HEAD_DIM
B_W = B_HEADS * HEAD_DIM
C_W = C_HEADS * HEAD_DIM
IN_COLS = 3 * A_W + 3 * B_W + 3 * C_W + C_HEADS
LANES = 128
IN_COLS_PAD = 3 * A_W + 3 * B_W + 3 * C_W + LANES

LOG2E = 1.4426950408889634
NEG = -1e30
VMEM_LIMIT = 56 * 2**20
BIAS_PIECES = 3

TM = 512
TQ = 512
DIL_TILE = 2048
BAND = 128
DIL_UNROLL = 8


def _cparams(*sem):
    return pltpu.CompilerParams(dimension_semantics=sem, vmem_limit_bytes=VMEM_LIMIT)


def _nt_dot(a, b):
    return lax.dot_general(a, b, (((1,), (1,)), ((), ())), preferred_element_type=F32)


def _mod_kernel(c_ref, w_ref, b_ref, o_ref):
    c = c_ref[...]
    sc = c * (1.0 / (1.0 + jnp.exp(-c)))
    o_ref[0] = jnp.dot(sc, w_ref[0], preferred_element_type=F32,
                       precision=lax.Precision.HIGHEST) + b_ref[0]


def _modulation(c, w_mod, b_mod):
    depth, d, n = w_mod.shape
    bn = c.shape[0]
    nb = n // d
    return pl.pallas_call(
        _mod_kernel,
        grid=(depth, nb),
        in_specs=[
            pl.BlockSpec((bn, d), lambda l, j: (0, 0)),
            pl.BlockSpec((1, d, d), lambda l, j: (l, 0, j)),
            pl.BlockSpec((1, 1, d), lambda l, j: (l, 0, j)),
        ],
        out_specs=pl.BlockSpec((1, bn, d), lambda l, j: (l, 0, j)),
        out_shape=jax.ShapeDtypeStruct((depth, bn, n), F32),
        compiler_params=_cparams("parallel", "parallel"),
        name="modulation",
    )(c, w_mod, b_mod.reshape(depth, 1, n))


def _modulated_norm(x, g, scale, shift):
    ms = jnp.mean(x * x, axis=-1, keepdims=True)
    y = x * lax.rsqrt(ms + NORM_EPS) * g
    return y * (1.0 + scale) + shift


def _rope128(r, cos, sin_signed, group):
    half = group // 2
    lane = lax.broadcasted_iota(jnp.int32, r.shape, 1)
    first = (lane % group) < half
    rot = jnp.where(first, pltpu.roll(r, LANES - half, 1), pltpu.roll(r, half, 1))
    return r * cos + rot * sin_signed


def _inproj_kernel(x_ref, g_ref, sc_ref, sh_ref, w_ref, cosa_ref, sina_ref, cosb_ref, sinb_ref,
                   qa_ref, ka_ref, va_ref, qb_ref, kb_ref, vb_ref, qc_ref, kc_ref, vc_ref, fz_ref):
    h = _modulated_norm(x_ref[0], g_ref[...], sc_ref[0], sh_ref[0]).astype(BF16)
    cosa, sina = cosa_ref[...], sina_ref[...]
    cosb, sinb = cosb_ref[...], sinb_ref[...]
    qa_scale = DIFF_QK_DIM ** -0.5 * LOG2E
    qb_scale = HEAD_DIM ** -0.5 * LOG2E
    qc_scale = HEAD_DIM ** -0.5 * LOG2E

    ra = jnp.dot(h, w_ref[:, 0:3 * A_W], preferred_element_type=F32)
    for blk in range(A_W // LANES):
        sl = slice(blk * LANES, (blk + 1) * LANES)
        q = ra[:, blk * LANES:(blk + 1) * LANES]
        k = ra[:, A_W + blk * LANES:A_W + (blk + 1) * LANES]
        qa_ref[0, :, sl] = (_rope128(q, cosa, sina, DIFF_QK_DIM) * qa_scale).astype(BF16)
        ka_ref[0, :, sl] = _rope128(k, cosa, sina, DIFF_QK_DIM).astype(BF16)
    va_ref[0] = ra[:, 2 * A_W:3 * A_W].astype(BF16)

    off = 3 * A_W
    rb = jnp.dot(h, w_ref[:, off:off + 3 * B_W], preferred_element_type=F32)
    for blk in range(B_W // LANES):
        sl = slice(blk * LANES, (blk + 1) * LANES)
        q = rb[:, blk * LANES:(blk + 1) * LANES]
        k = rb[:, B_W + blk * LANES:B_W + (blk + 1) * LANES]
        qb_ref[0, :, sl] = _rope128(q, cosb, sinb, HEAD_DIM) * qb_scale
        kb_ref[0, :, sl] = _rope128(k, cosb, sinb, HEAD_DIM)
    vb_ref[0] = rb[:, 2 * B_W:3 * B_W]

    off = 3 * A_W + 3 * B_W
    rc = jnp.dot(h, w_ref[:, off:off + 3 * C_W + LANES], preferred_element_type=F32)
    qc_ref[0] = (rc[:, 0:C_W] * qc_scale).astype(BF16)
    kc_ref[0] = rc[:, C_W:2 * C_W].astype(BF16)
    vc_ref[0] = rc[:, 2 * C_W:3 * C_W].astype(BF16)
    fz_ref[0] = rc[:, 3 * C_W:3 * C_W + LANES]


def _in_projection(x, g, scale, shift, w_in_p, tabs):
    bn, t, d = x.shape
    cosa, sina, cosb, sinb = tabs
    tok = lambda w: pl.BlockSpec((1, TM, w), lambda b, i: (b, i, 0))
    per_b = pl.BlockSpec((1, 1, d), lambda b, i: (b, 0, 0))
    tab = pl.BlockSpec((TM, LANES), lambda b, i: (i, 0))
    widths = [A_W] * 3 + [B_W] * 3 + [C_W] * 3 + [LANES]
    dtypes = [BF16] * 3 + [F32] * 3 + [BF16] * 3 + [F32]
    return pl.pallas_call(
        _inproj_kernel,
        grid=(bn, t // TM),
        in_specs=[tok(d), pl.BlockSpec((1, d), lambda b, i: (0, 0)), per_b, per_b,
                  pl.BlockSpec((d, IN_COLS_PAD), lambda b, i: (0, 0)), tab, tab, tab, tab],
        out_specs=[tok(w) for w in widths],
        out_shape=[jax.ShapeDtypeStruct((bn, t, w), dt) for w, dt in zip(widths, dtypes)],
        compiler_params=_cparams("parallel", "parallel"),
        name="in_projection",
    )(x, g.reshape(1, d), scale, shift, w_in_p, cosa, sina, cosb, sinb)


def _cumlogf_kernel(z_ref, b_ref, u_ref, lb_ref, o_ref):
    z = z_ref[0] + b_ref[...]
    logf = jnp.minimum(z, 0.0) - jnp.log1p(jnp.exp(-jnp.abs(z)))
    hi = lax.Precision.HIGHEST
    within = jnp.dot(logf, u_ref[...], preferred_element_type=F32, precision=hi)
    before = jnp.dot(lb_ref[...], within, preferred_element_type=F32, precision=hi)
    cum = (within + before[:, LANES - 1:LANES]) * LOG2E
    rest = cum
    for piece in range(BIAS_PIECES):
        part = rest.astype(BF16).astype(F32)
        o_ref[0, piece] = part
        rest = rest - part


def _cum_logf(fz, forget_bias):
    bn, t, _ = fz.shape
    nc = t // LANES
    rows = 8 * nc
    z = jnp.transpose(fz[:, :, :8], (0, 2, 1)).reshape(bn, rows, LANES)
    bias = jnp.concatenate([forget_bias.astype(F32), jnp.zeros((8 - C_HEADS,), F32)])
    bias = jnp.broadcast_to(jnp.repeat(bias, nc)[:, None], (rows, LANES))
    idx = jnp.arange(LANES)
    upper = (idx[:, None] <= idx[None, :]).astype(F32)
    r = jnp.arange(rows)
    lblk = ((r[:, None] // nc == r[None, :] // nc) & (r[None, :] < r[:, None])).astype(F32)
    cum = pl.pallas_call(
        _cumlogf_kernel,
        grid=(bn,),
        in_specs=[pl.BlockSpec((1, rows, LANES), lambda b: (b, 0, 0)),
                  pl.BlockSpec((rows, LANES), lambda b: (0, 0)),
                  pl.BlockSpec((LANES, LANES), lambda b: (0, 0)),
                  pl.BlockSpec((rows, rows), lambda b: (0, 0))],
        out_specs=pl.BlockSpec((1, BIAS_PIECES, rows, LANES), lambda b: (b, 0, 0, 0)),
        out_shape=jax.ShapeDtypeStruct((bn, BIAS_PIECES, rows, LANES), F32),
        compiler_params=_cparams("parallel"),
        name="cum_logf",
    )(z, bias, upper, lblk)
    return cum.reshape(bn, BIAS_PIECES, 8, t)


def _bias_operands(cum):
    bn, _, _, t = cum.shape
    pieces = jnp.transpose(cum[:, :, :C_HEADS], (0, 3, 2, 1)).reshape(bn, t, C_HEADS // 2, 2, BIAS_PIECES)
    ones = jnp.ones_like(pieces)
    pad = jnp.zeros(pieces.shape[:-1] + (8 - 2 * BIAS_PIECES,), F32)
    lane_pad = ((0, 0), (0, 0), (0, 0), (0, LANES - 16))

    def pack(first, second):
        blk = jnp.concatenate([first, second, pad], axis=-1).reshape(bn, t, C_HEADS // 2, 16)
        return jnp.pad(blk, lane_pad).reshape(bn, t, C_W).astype(BF16)

    return pack(ones, pieces), pack(-pieces, ones)


def _lane_blocks(s):
    return [s[:, cb * LANES:(cb + 1) * LANES] for cb in range(s.shape[1] // LANES)]


def _lane_range(lane, lo, hi):
    return (lane >= lo) & (lane < hi)


def _own_half(lane, h):
    return lane < HEAD_DIM if h == 0 else lane >= HEAD_DIM


def _with_ones(v2, lane, h):
    return jnp.where(_own_half(lane, h), v2, jnp.ones_like(v2))


def _online_softmax(s, keep, m_prev):
    blocks = _lane_blocks(s)
    if keep is not None:
        blocks = [jnp.where(kb, sb, NEG) for kb, sb in zip(keep, blocks)]
    mpart = blocks[0]
    for sb in blocks[1:]:
        mpart = jnp.maximum(mpart, sb)
    m_new = jnp.maximum(m_prev, jnp.max(mpart, axis=-1, keepdims=True))
    alpha = jnp.exp2(m_prev - m_new)
    p = jnp.concatenate([jnp.exp2(sb - m_new).astype(BF16) for sb in blocks], axis=1)
    return m_new, alpha, p


def _causal_keep(tq):
    row = lax.broadcasted_iota(jnp.int32, (tq, LANES), 0)
    lane = lax.broadcasted_iota(jnp.int32, (tq, LANES), 1)
    return [lane + cb * LANES <= row for cb in range(tq // LANES)]


def _normalize(acc):
    return acc / pltpu.roll(acc, HEAD_DIM, 1)


def _fox_kernel(q_ref, qx_ref, k_ref, kx_ref, v_ref, o_ref, m_sc, acc_sc, *, tq):
    i = pl.program_id(2)
    lane = lax.broadcasted_iota(jnp.int32, (tq, LANES), 1)
    q2, qx = q_ref[0], qx_ref[0]
    zero = jnp.zeros_like(q2)
    qcat = jnp.concatenate(
        [jnp.concatenate([jnp.where(_own_half(lane, h), q2, zero),
                          jnp.where(_lane_range(lane, 8 * h, 8 * h + 8), qx, zero)], axis=1)
         for h in range(2)], axis=0)
    m_sc[...] = jnp.full(m_sc.shape, NEG, F32)
    acc_sc[...] = jnp.zeros(acc_sc.shape, F32)

    def step(j, keep):
        rows = pl.ds(pl.multiple_of(j * tq, tq), tq)
        kcat = jnp.concatenate([k_ref[0, rows, :], kx_ref[0, rows, :]], axis=1)
        v2 = v_ref[0, rows, :]
        s = _nt_dot(qcat, kcat)
        for h in range(2):
            m_new, alpha, p = _online_softmax(s[h * tq:(h + 1) * tq], keep, m_sc[h])
            pv = jnp.dot(p, _with_ones(v2, lane, h), preferred_element_type=F32)
            acc_sc[h] = alpha * acc_sc[h] + pv
            m_sc[h] = m_new

    def body(j, carry):
        step(j, None)
        return carry

    lax.fori_loop(0, i, body, 0)
    step(i, _causal_keep(tq))

    o_ref[0] = jnp.where(lane < HEAD_DIM, _normalize(acc_sc[0]), _normalize(acc_sc[1])).astype(o_ref.dtype)


def _fox_attention(qc, kc, vc, q_aug, k_aug):
    bn, t, _ = qc.shape
    tq = TQ
    qblk = pl.BlockSpec((1, tq, LANES), lambda b, p, i: (b, i, p))
    full = pl.BlockSpec((1, t, LANES), lambda b, p, i: (b, 0, p))
    return pl.pallas_call(
        functools.partial(_fox_kernel, tq=tq),
        grid=(bn, C_HEADS // 2, t // tq),
        in_specs=[qblk, qblk, full, full, full],
        out_specs=qblk,
        out_shape=jax.ShapeDtypeStruct((bn, t, C_W), BF16),
        scratch_shapes=[pltpu.VMEM((2, tq, LANES), F32)] * 2,
        compiler_params=_cparams("parallel", "parallel", "arbitrary"),
        name="forgetting_attention",
    )(qc, q_aug, kc, k_aug, vc)


def _diff_kernel(q_ref, k_ref, v_ref, lp_ref, g_ref, o_ref, m_sc, acc_sc, *, tq, lam_init):
    i = pl.program_id(2)
    lane = lax.broadcasted_iota(jnp.int32, (tq, LANES), 1)
    q2 = q_ref[0]
    zero = jnp.zeros_like(q2)
    nmap = LANES // DIFF_QK_DIM
    qcat = jnp.concatenate(
        [jnp.where(_lane_range(lane, g * DIFF_QK_DIM, (g + 1) * DIFF_QK_DIM), q2, zero)
         for g in range(nmap)], axis=0)
    m_sc[...] = jnp.full(m_sc.shape, NEG, F32)
    acc_sc[...] = jnp.zeros(acc_sc.shape, F32)

    def step(j, keep):
        rows = pl.ds(pl.multiple_of(j * tq, tq), tq)
        v2 = v_ref[0, rows, :]
        s = _nt_dot(qcat, k_ref[0, rows, :])
        for h in range(2):
            stats = [_online_softmax(s[g * tq:(g + 1) * tq], keep, m_sc[g]) for g in (2 * h, 2 * h + 1)]
            p = jnp.concatenate([st[2] for st in stats], axis=0)
            pv = jnp.dot(p, _with_ones(v2, lane, h), preferred_element_type=F32)
            for n, g in enumerate((2 * h, 2 * h + 1)):
                acc_sc[g] = stats[n][1] * acc_sc[g] + pv[n * tq:(n + 1) * tq]
                m_sc[g] = stats[n][0]

    def body(j, carry):
        step(j, None)
        return carry

    lax.fori_loop(0, i, body, 0)
    step(i, _causal_keep(tq))

    lp = lp_ref[...]
    lam = (jnp.exp(jnp.sum(lp[0:1] * lp[1:2], axis=-1, keepdims=True))
           - jnp.exp(jnp.sum(lp[2:3] * lp[3:4], axis=-1, keepdims=True)) + lam_init)
    norm = [_normalize(acc_sc[g]) for g in range(nmap)]
    first = lane < HEAD_DIM
    o = jnp.where(first, norm[0] - lam * norm[1], norm[2] - lam * norm[3])
    osq = o * o
    ms0 = jnp.sum(jnp.where(first, osq, 0.0), axis=-1, keepdims=True) * (1.0 / HEAD_DIM)
    ms1 = jnp.sum(jnp.where(first, 0.0, osq), axis=-1, keepdims=True) * (1.0 / HEAD_DIM)
    o = o * lax.rsqrt(jnp.where(first, ms0, ms1) + SUBLN_EPS)
    o_ref[0] = (o * g_ref[...] * (1.0 - lam_init)).astype(o_ref.dtype)


def _diff_attention(qa, ka, va, lam_params, subln_g, lam_init):
    bn, t, _ = qa.shape
    tq = TQ
    g2 = jnp.concatenate([subln_g, subln_g]).reshape(1, LANES).astype(F32)
    qblk = pl.BlockSpec((1, tq, LANES), lambda b, p, i: (b, i, p))
    full = pl.BlockSpec((1, t, LANES), lambda b, p, i: (b, 0, p))
    return pl.pallas_call(
        functools.partial(_diff_kernel, tq=tq, lam_init=lam_init),
        grid=(bn, A_HEADS // 2, t // tq),
        in_specs=[qblk, full, full,
                  pl.BlockSpec((4, DIFF_QK_DIM), lambda b, p, i: (0, 0)),
                  pl.BlockSpec((1, LANES), lambda b, p, i: (0, 0))],
        out_specs=qblk,
        out_shape=jax.ShapeDtypeStruct((bn, t, A_W), BF16),
        scratch_shapes=[pltpu.VMEM((4, tq, LANES), F32)] * 2,
        compiler_params=_cparams("parallel", "parallel", "arbitrary"),
        name="diff_attention",
    )(qa, ka, va, lam_params.astype(F32), g2)


def _dilated_kernel(q_ref, k_ref, v_ref, o_ref, m_sc, acc_sc, *, tile):
    base = pl.program_id(2) * tile
    row = lax.broadcasted_iota(jnp.int32, (BAND, LANES), 0)
    lane = lax.broadcasted_iota(jnp.int32, (BAND, LANES), 1)
    lane2 = lax.broadcasted_iota(jnp.int32, (2 * BAND, LANES), 1)
    bias_prev = jnp.where(lane >= row, 0.0, NEG)
    bias_cur = jnp.where(lane <= row, 0.0, NEG)
    none_prev = jnp.full((BAND, LANES), NEG, F32)

    for pat, (window, d) in enumerate(DILATED_PATTERNS):
        span = BAND * d
        assert window == span and tile % span == 0

        def rows(start, d=d):
            return pl.ds(start, BAND) if d == 1 else pl.ds(start, BAND, stride=d)

        def body(idx, carry, d=d, span=span, rows=rows, pat=pat):
            u = idx // d
            r = idx - u * d
            qs = u * span + r
            cur = base + qs
            prev = cur - span
            has_prev = prev >= 0
            prev = jnp.where(has_prev, prev, cur)
            q = q_ref[0, rows(qs), :].astype(BF16)
            zero = jnp.zeros_like(q)
            qcat = jnp.concatenate([jnp.where(_own_half(lane, h), q, zero) for h in range(2)], axis=0)
            kcat = jnp.concatenate([k_ref[0, rows(prev), :], k_ref[0, rows(cur), :]], axis=0).astype(BF16)
            vcat = jnp.concatenate([v_ref[0, rows(prev), :], v_ref[0, rows(cur), :]], axis=0).astype(BF16)
            bias = jnp.concatenate([jnp.where(has_prev, bias_prev, none_prev), bias_cur], axis=1)
            s = _nt_dot(qcat, kcat)
            for h in range(2):
                sh = s[h * BAND:(h + 1) * BAND] + bias
                m = jnp.broadcast_to(jnp.max(sh, axis=-1, keepdims=True), (BAND, LANES))
                p = jnp.concatenate([jnp.exp2(sb - m).astype(BF16) for sb in _lane_blocks(sh)], axis=1)
                pv = jnp.dot(p, _with_ones(vcat, lane2, h), preferred_element_type=F32)
                acc_sc[2 * pat + h, rows(qs), :] = pv
                m_sc[2 * pat + h, rows(qs), :] = m
            return carry

        lax.fori_loop(0, tile // BAND, body, 0, unroll=DIL_UNROLL)

    lane_t = lax.broadcasted_iota(jnp.int32, (tile, LANES), 1)
    npat = len(DILATED_PATTERNS)
    outs = []
    for h in range(2):
        ms = [m_sc[2 * pat + h] for pat in range(npat)]
        m_all = functools.reduce(jnp.maximum, ms)
        tot = sum(jnp.exp2(ms[pat] - m_all) * acc_sc[2 * pat + h] for pat in range(npat))
        outs.append(_normalize(tot))
    o_ref[0] = jnp.where(lane_t < HEAD_DIM, outs[0], outs[1]).astype(o_ref.dtype)


def _dilated_attention(qb, kb, vb):
    bn, t, _ = qb.shape
    tile = DIL_TILE
    nstate = 2 * len(DILATED_PATTERNS)
    full = pl.BlockSpec((1, t, LANES), lambda b, p, i: (b, 0, p))
    return pl.pallas_call(
        functools.partial(_dilated_kernel, tile=tile),
        grid=(bn, B_HEADS // 2, t // tile),
        in_specs=[pl.BlockSpec((1, tile, LANES), lambda b, p, i: (b, i, p)), full, full],
        out_specs=pl.BlockSpec((1, tile, LANES), lambda b, p, i: (b, i, p)),
        out_shape=jax.ShapeDtypeStruct((bn, t, B_W), BF16),
        scratch_shapes=[pltpu.VMEM((nstate, tile, LANES), F32)] * 2,
        compiler_params=_cparams("parallel", "parallel", "arbitrary"),
        name="dilated_attention",
    )(qb, kb, vb)


def _outproj_kernel(oa_ref, ob_ref, oc_ref, x_ref, w_ref, gate_ref, g_ref, sc_ref, sh_ref, x1_ref, h2_ref):
    o = jnp.concatenate([oa_ref[0], ob_ref[0], oc_ref[0]], axis=-1)
    x1 = x_ref[0] + gate_ref[0] * jnp.dot(o, w_ref[...], preferred_element_type=F32)
    x1_ref[0] = x1
    h2_ref[0] = _modulated_norm(x1, g_ref[...], sc_ref[0], sh_ref[0]).astype(BF16)


def _out_projection(oa, ob, oc, x, w_out, gate1, g_mlp, scale2, shift2):
    bn, t, d = x.shape
    tok = lambda w: pl.BlockSpec((1, TM, w), lambda b, i: (b, i, 0))
    per_b = pl.BlockSpec((1, 1, d), lambda b, i: (b, 0, 0))
    return pl.pallas_call(
        _outproj_kernel,
        grid=(bn, t // TM),
        in_specs=[tok(A_W), tok(B_W), tok(C_W), tok(d),
                  pl.BlockSpec((d, d), lambda b, i: (0, 0)), per_b,
                  pl.BlockSpec((1, d), lambda b, i: (0, 0)), per_b, per_b],
        out_specs=[tok(d), tok(d)],
        out_shape=[jax.ShapeDtypeStruct((bn, t, d), F32), jax.ShapeDtypeStruct((bn, t, d), BF16)],
        compiler_params=_cparams("parallel", "parallel"),
        name="out_projection",
    )(oa, ob, oc, x, w_out, gate1, g_mlp.reshape(1, d), scale2, shift2)


FF_CHUNK = 256
HALO = 8


def _up_kernel(h_ref, w_ref, cw_ref, cb_ref, y_ref, halo_sc, g_sc):
    i = pl.program_id(1)

    @pl.when(i == 0)
    def _():
        halo_sc[...] = jnp.zeros(halo_sc.shape, F32)

    h = h_ref[0]
    tm = h.shape[0]
    for c in range(D_FF // FF_CHUNK):
        cs = slice(c * FF_CHUNK, (c + 1) * FF_CHUNK)
        u = jnp.dot(h, w_ref[:, cs], preferred_element_type=F32)
        g = jnp.dot(h, w_ref[:, D_FF + c * FF_CHUNK:D_FF + (c + 1) * FF_CHUNK], preferred_element_type=F32)
        g_sc[0:HALO, :] = halo_sc[:, cs]
        g_sc[HALO:HALO + tm, :] = g
        halo_sc[:, cs] = g[tm - HALO:tm, :]
        conv = (cb_ref[:, cs] + g_sc[HALO - 2:HALO - 2 + tm, :] * cw_ref[0:1, cs]
                + g_sc[HALO - 1:HALO - 1 + tm, :] * cw_ref[1:2, cs] + g * cw_ref[2:3, cs])
        act = conv * (1.0 / (1.0 + jnp.exp(-conv)))
        y_ref[0, :, cs] = (act * u).astype(BF16)


def _up_projection(h2, w_up, conv_w, conv_b):
    bn, t, d = h2.shape
    return pl.pallas_call(
        _up_kernel,
        grid=(bn, t // TM),
        in_specs=[pl.BlockSpec((1, TM, d), lambda b, i: (b, i, 0)),
                  pl.BlockSpec((d, 2 * D_FF), lambda b, i: (0, 0)),
                  pl.BlockSpec((CONV_WIDTH, D_FF), lambda b, i: (0, 0)),
                  pl.BlockSpec((1, D_FF), lambda b, i: (0, 0))],
        out_specs=pl.BlockSpec((1, TM, D_FF), lambda b, i: (b, i, 0)),
        out_shape=jax.ShapeDtypeStruct((bn, t, D_FF), BF16),
        scratch_shapes=[pltpu.VMEM((HALO, D_FF), F32), pltpu.VMEM((HALO + TM, FF_CHUNK), F32)],
        compiler_params=_cparams("parallel", "arbitrary"),
        name="up_projection",
    )(h2, w_up, conv_w, conv_b.reshape(1, D_FF))


def _down_kernel(y_ref, x_ref, w_ref, gate_ref, g_ref, o_ref, *, final_norm):
    x2 = x_ref[0] + gate_ref[0] * jnp.dot(y_ref[0], w_ref[...], preferred_element_type=F32)
    if final_norm:
        ms = jnp.mean(x2 * x2, axis=-1, keepdims=True)
        x2 = x2 * lax.rsqrt(ms + NORM_EPS) * g_ref[...]
    o_ref[0] = x2


def _down_projection(y, x1, w_down, gate2, g_final, final_norm):
    bn, t, d = x1.shape
    return pl.pallas_call(
        functools.partial(_down_kernel, final_norm=final_norm),
        grid=(bn, t // TM),
        in_specs=[pl.BlockSpec((1, TM, D_FF), lambda b, i: (b, i, 0)),
                  pl.BlockSpec((1, TM, d), lambda b, i: (b, i, 0)),
                  pl.BlockSpec((D_FF, d), lambda b, i: (0, 0)),
                  pl.BlockSpec((1, 1, d), lambda b, i: (b, 0, 0)),
                  pl.BlockSpec((1, d), lambda b, i: (0, 0))],
        out_specs=pl.BlockSpec((1, TM, d), lambda b, i: (b, i, 0)),
        out_shape=jax.ShapeDtypeStruct((bn, t, d), F32),
        compiler_params=_cparams("parallel", "parallel"),
        name="down_projection",
    )(y, x1, w_down, gate2, g_final.reshape(1, d))


def _rope_table(t, dim):
    inv = 1.0 / (ROPE_THETA ** (jnp.arange(0, dim, 2, dtype=F32) / dim))
    ang = jnp.arange(t, dtype=F32)[:, None] * inv[None, :]
    cos, sin = jnp.cos(ang), jnp.sin(ang)
    reps = LANES // dim
    return (jnp.tile(jnp.concatenate([cos, cos], axis=-1), (1, reps)),
            jnp.tile(jnp.concatenate([-sin, sin], axis=-1), (1, reps)))


def kernel(x, c, w_mod, b_mod, g_attn, w_in, diff_lambda, subln_g, forget_bias, w_out,
           g_mlp, w_up, conv_w, conv_b, w_down, g_final):
    depth = w_mod.shape[0]
    bn, t, d = x.shape
    assert d == D_MODEL and t % DIL_TILE == 0 and t % TM == 0 and t % TQ == 0
    tabs = _rope_table(t, DIFF_QK_DIM) + _rope_table(t, HEAD_DIM)
    mod = _modulation(c, w_mod, b_mod)
    w_in_p = jnp.pad(w_in, ((0, 0), (0, 0), (0, IN_COLS_PAD - IN_COLS))).astype(BF16)
    w_out_b, w_up_b, w_down_b = w_out.astype(BF16), w_up.astype(BF16), w_down.astype(BF16)

    for layer in range(depth):
        lam_init = 0.8 - 0.6 * math.exp(-0.3 * layer)
        shift1, scale1, gate1, shift2, scale2, gate2 = [
            m[:, None, :] for m in jnp.split(mod[layer], 6, axis=-1)]
        qa, ka, va, qb, kb, vb, qc, kc, vc, fz = _in_projection(
            x, g_attn[layer], scale1, shift1, w_in_p[layer], tabs)
        q_aug, k_aug = _bias_operands(_cum_logf(fz, forget_bias[layer]))
        oa = _diff_attention(qa, ka, va, diff_lambda[layer], subln_g[layer], lam_init)
        ob = _dilated_attention(qb, kb, vb)
        oc = _fox_attention(qc, kc, vc, q_aug, k_aug)
        x1, h2 = _out_projection(oa, ob, oc, x, w_out_b[layer], gate1, g_mlp[layer], scale2, shift2)
        y = _up_projection(h2, w_up_b[layer], conv_w[layer], conv_b[layer])
        x = _down_projection(y, x1, w_down_b[layer], gate2, g_final, layer == depth - 1)
    return x
```

```python
import functools
import math

import jax
import jax.numpy as jnp
from jax import lax
from jax.experimental import pallas as pl
from jax.experimental.pallas import tpu as pltpu

F32 = jnp.float32
BF16 = jnp.bfloat16

D_MODEL = 1024
HEAD_DIM = 64
A_HEADS = 4
DIFF_QK_DIM = 32
B_HEADS = 6
C_HEADS = 6
DILATED_PATTERNS = ((128, 1), (512, 4), (2048, 16))
ROPE_THETA = 10000.0
D_FF = 2816
CONV_WIDTH = 3
NORM_EPS = 1e-6
SUBLN_EPS = 1e-5

A_W = A_HEADS * HEAD_DIM
B_W = B_HEADS * HEAD_DIM
C_W = C_HEADS * HEAD_DIM
IN_COLS = 3 * A_W + 3 * B_W + 3 * C_W + C_HEADS
LANES = 128
IN_COLS_PAD = 3 * A_W + 3 * B_W + 3 * C_W + LANES

LOG2E = 1.4426950408889634
NEG = -1e30
VMEM_LIMIT = 56 * 2**20
BIAS_PIECES = 3

TM = 512
TQ = 512
DIL_TILE = 2048
BAND = 128
NCLS = DIL_TILE // BAND
DIL_UNROLL = 8


def _cparams(*sem):
    return pltpu.CompilerParams(dimension_semantics=sem, vmem_limit_bytes=VMEM_LIMIT)


def _nt_dot(a, b):
    return lax.dot_general(a, b, (((1,), (1,)), ((), ())), preferred_element_type=F32)


def _mod_kernel(c_ref, w_ref, b_ref, o_ref):
    c = c_ref[...]
    sc = c * (1.0 / (1.0 + jnp.exp(-c)))
    o_ref[0] = jnp.dot(sc, w_ref[0], preferred_element_type=F32,
                       precision=lax.Precision.HIGHEST) + b_ref[0]


def _modulation(c, w_mod, b_mod):
    depth, d, n = w_mod.shape
    bn = c.shape[0]
    nb = n // d
    return pl.pallas_call(
        _mod_kernel,
        grid=(depth, nb),
        in_specs=[
            pl.BlockSpec((bn, d), lambda l, j: (0, 0)),
            pl.BlockSpec((1, d, d), lambda l, j: (l, 0, j)),
            pl.BlockSpec((1, 1, d), lambda l, j: (l, 0, j)),
        ],
        out_specs=pl.BlockSpec((1, bn, d), lambda l, j: (l, 0, j)),
        out_shape=jax.ShapeDtypeStruct((depth, bn, n), F32),
        compiler_params=_cparams("parallel", "parallel"),
        name="modulation",
    )(c, w_mod, b_mod.reshape(depth, 1, n))


def _modulated_norm(x, g, scale, shift):
    ms = jnp.mean(x * x, axis=-1, keepdims=True)
    y = x * lax.rsqrt(ms + NORM_EPS) * g
    return y * (1.0 + scale) + shift


def _rope128(r, cos, sin_signed, group):
    half = group // 2
    lane = lax.broadcasted_iota(jnp.int32, r.shape, 1)
    first = (lane % group) < half
    rot = jnp.where(first, pltpu.roll(r, LANES - half, 1), pltpu.roll(r, half, 1))
    return r * cos + rot * sin_signed


def _inproj_kernel(x_ref, g_ref, sc_ref, sh_ref, w_ref, cosa_ref, sina_ref, cosb_ref, sinb_ref,
                   qa_ref, ka_ref, va_ref, qb_ref, kb_ref, vb_ref, qc_ref, kc_ref, vc_ref, fz_ref):
    h = _modulated_norm(x_ref[0], g_ref[...], sc_ref[0], sh_ref[0]).astype(BF16)
    cosa, sina = cosa_ref[...], sina_ref[...]
    cosb, sinb = cosb_ref[...], sinb_ref[...]
    qa_scale = DIFF_QK_DIM ** -0.5 * LOG2E
    qb_scale = HEAD_DIM ** -0.5 * LOG2E
    qc_scale = HEAD_DIM ** -0.5 * LOG2E

    ra = jnp.dot(h, w_ref[:, 0:3 * A_W], preferred_element_type=F32)
    for blk in range(A_W // LANES):
        sl = slice(blk * LANES, (blk + 1) * LANES)
        q = ra[:, blk * LANES:(blk + 1) * LANES]
        k = ra[:, A_W + blk * LANES:A_W + (blk + 1) * LANES]
        qa_ref[0, :, sl] = (_rope128(q, cosa, sina, DIFF_QK_DIM) * qa_scale).astype(BF16)
        ka_ref[0, :, sl] = _rope128(k, cosa, sina, DIFF_QK_DIM).astype(BF16)
    va_ref[0] = ra[:, 2 * A_W:3 * A_W].astype(BF16)

    off = 3 * A_W
    rb = jnp.dot(h, w_ref[:, off:off + 3 * B_W], preferred_element_type=F32)
    for blk in range(B_W // LANES):
        sl = slice(blk * LANES, (blk + 1) * LANES)
        q = rb[:, blk * LANES:(blk + 1) * LANES]
        k = rb[:, B_W + blk * LANES:B_W + (blk + 1) * LANES]
        qb_ref[0, :, sl] = _rope128(q, cosb, sinb, HEAD_DIM) * qb_scale
        kb_ref[0, :, sl] = _rope128(k, cosb, sinb, HEAD_DIM)
    vb_ref[0] = rb[:, 2 * B_W:3 * B_W]

    off = 3 * A_W + 3 * B_W
    rc = jnp.dot(h, w_ref[:, off:off + 3 * C_W + LANES], preferred_element_type=F32)
    qc_ref[0] = (rc[:, 0:C_W] * qc_scale).astype(BF16)
    kc_ref[0] = rc[:, C_W:2 * C_W].astype(BF16)
    vc_ref[0] = rc[:, 2 * C_W:3 * C_W].astype(BF16)
    fz_ref[0] = rc[:, 3 * C_W:3 * C_W + LANES]


def _in_projection(x, g, scale, shift, w_in_p, tabs):
    bn, t, d = x.shape
    cosa, sina, cosb, sinb = tabs
    tok = lambda w: pl.BlockSpec((1, TM, w), lambda b, i: (b, i, 0))
    per_b = pl.BlockSpec((1, 1, d), lambda b, i: (b, 0, 0))
    tab = pl.BlockSpec((TM, LANES), lambda b, i: (i, 0))
    widths = [A_W] * 3 + [B_W] * 3 + [C_W] * 3 + [LANES]
    dtypes = [BF16] * 3 + [F32] * 3 + [BF16] * 3 + [F32]
    return pl.pallas_call(
        _inproj_kernel,
        grid=(bn, t // TM),
        in_specs=[tok(d), pl.BlockSpec((1, d), lambda b, i: (0, 0)), per_b, per_b,
                  pl.BlockSpec((d, IN_COLS_PAD), lambda b, i: (0, 0)), tab, tab, tab, tab],
        out_specs=[tok(w) for w in widths],
        out_shape=[jax.ShapeDtypeStruct((bn, t, w), dt) for w, dt in zip(widths, dtypes)],
        compiler_params=_cparams("parallel", "parallel"),
        name="in_projection",
    )(x, g.reshape(1, d), scale, shift, w_in_p, cosa, sina, cosb, sinb)


def _cumlogf_kernel(z_ref, b_ref, u_ref, lb_ref, o_ref):
    z = z_ref[0] + b_ref[...]
    logf = jnp.minimum(z, 0.0) - jnp.log1p(jnp.exp(-jnp.abs(z)))
    hi = lax.Precision.HIGHEST
    within = jnp.dot(logf, u_ref[...], preferred_element_type=F32, precision=hi)
    before = jnp.dot(lb_ref[...], within, preferred_element_type=F32, precision=hi)
    cum = (within + before[:, LANES - 1:LANES]) * LOG2E
    rest = cum
    for piece in range(BIAS_PIECES):
        part = rest.astype(BF16).astype(F32)
        o_ref[0, piece] = part
        rest = rest - part


def _cum_logf(fz, forget_bias):
    bn, t, _ = fz.shape
    nc = t // LANES
    rows = 8 * nc
    z = jnp.transpose(fz[:, :, :8], (0, 2, 1)).reshape(bn, rows, LANES)
    bias = jnp.concatenate([forget_bias.astype(F32), jnp.zeros((8 - C_HEADS,), F32)])
    bias = jnp.broadcast_to(jnp.repeat(bias, nc)[:, None], (rows, LANES))
    idx = jnp.arange(LANES)
    upper = (idx[:, None] <= idx[None, :]).astype(F32)
    r = jnp.arange(rows)
    lblk = ((r[:, None] // nc == r[None, :] // nc) & (r[None, :] < r[:, None])).astype(F32)
    cum = pl.pallas_call(
        _cumlogf_kernel,
        grid=(bn,),
        in_specs=[pl.BlockSpec((1, rows, LANES), lambda b: (b, 0, 0)),
                  pl.BlockSpec((rows, LANES), lambda b: (0, 0)),
                  pl.BlockSpec((LANES, LANES), lambda b: (0, 0)),
                  pl.BlockSpec((rows, rows), lambda b: (0, 0))],
        out_specs=pl.BlockSpec((1, BIAS_PIECES, rows, LANES), lambda b: (b, 0, 0, 0)),
        out_shape=jax.ShapeDtypeStruct((bn, BIAS_PIECES, rows, LANES), F32),
        compiler_params=_cparams("parallel"),
        name="cum_logf",
    )(z, bias, upper, lblk)
    return cum.reshape(bn, BIAS_PIECES, 8, t)


def _bias_operands(cum):
    bn, _, _, t = cum.shape
    pieces = jnp.transpose(cum[:, :, :C_HEADS], (0, 3, 2, 1)).reshape(bn, t, C_HEADS // 2, 2, BIAS_PIECES)
    ones = jnp.ones_like(pieces)
    pad = jnp.zeros(pieces.shape[:-1] + (8 - 2 * BIAS_PIECES,), F32)
    lane_pad = ((0, 0), (0, 0), (0, 0), (0, LANES - 16))

    def pack(first, second):
        blk = jnp.concatenate([first, second, pad], axis=-1).reshape(bn, t, C_HEADS // 2, 16)
        return jnp.pad(blk, lane_pad).reshape(bn, t, C_W).astype(BF16)

    return pack(ones, pieces), pack(-pieces, ones)


def _lane_blocks(s):
    return [s[:, cb * LANES:(cb + 1) * LANES] for cb in range(s.shape[1] // LANES)]


def _lane_range(lane, lo, hi):
    return (lane >= lo) & (lane < hi)


def _own_half(lane, h):
    return lane < HEAD_DIM if h == 0 else lane >= HEAD_DIM


def _with_ones(v2, lane, h):
    return jnp.where(_own_half(lane, h), v2, jnp.ones_like(v2))


def _online_softmax(s, keep, m_prev):
    blocks = _lane_blocks(s)
    if keep is not None:
        blocks = [jnp.where(kb, sb, NEG) for kb, sb in zip(keep, blocks)]
    mpart = blocks[0]
    for sb in blocks[1:]:
        mpart = jnp.maximum(mpart, sb)
    m_new = jnp.maximum(m_prev, jnp.max(mpart, axis=-1, keepdims=True))
    alpha = jnp.exp2(m_prev - m_new)
    p = jnp.concatenate([jnp.exp2(sb - m_new).astype(BF16) for sb in blocks], axis=1)
    return m_new, alpha, p


def _causal_keep(tq):
    row = lax.broadcasted_iota(jnp.int32, (tq, LANES), 0)
    lane = lax.broadcasted_iota(jnp.int32, (tq, LANES), 1)
    return [lane + cb * LANES <= row for cb in range(tq // LANES)]


def _causal_sweep(i, scores, consume, tq, group, scores_first):
    def run(j0, n):
        if scores_first:
            tiles = [scores(j0 + k) for k in range(n)]
            for k in range(n):
                consume(tiles[k], j0 + k, None)
        else:
            for k in range(n):
                consume(scores(j0 + k), j0 + k, None)

    def body(p, carry):
        run(group * p, group)
        return carry

    shift = group.bit_length() - 1
    assert group == 1 << shift
    lax.fori_loop(0, lax.shift_right_logical(i, shift), body, 0)
    half = group // 2
    while half >= 1:
        done = i & ~(2 * half - 1)

        @pl.when((i & half) != 0)
        def _(done=done, half=half):
            run(done, half)

        half //= 2

    consume(scores(i), i, _causal_keep(tq))


def _normalize(acc):
    return acc / pltpu.roll(acc, HEAD_DIM, 1)


def _fox_kernel(q_ref, qx_ref, k_ref, kx_ref, v_ref, o_ref, m_sc, acc_sc, *, tq):
    i = pl.program_id(2)
    lane = lax.broadcasted_iota(jnp.int32, (tq, LANES), 1)
    q2, qx = q_ref[0], qx_ref[0]
    zero = jnp.zeros_like(q2)
    qcat = jnp.concatenate(
        [jnp.concatenate([jnp.where(_own_half(lane, h), q2, zero),
                          jnp.where(_lane_range(lane, 8 * h, 8 * h + 8), qx, zero)], axis=1)
         for h in range(2)], axis=0)
    m_sc[...] = jnp.full(m_sc.shape, NEG, F32)
    acc_sc[...] = jnp.zeros(acc_sc.shape, F32)

    def key_rows(j):
        return pl.ds(pl.multiple_of(j * tq, tq), tq)

    def scores(j):
        kcat = jnp.concatenate([k_ref[0, key_rows(j), :], kx_ref[0, key_rows(j), :]], axis=1)
        return _nt_dot(qcat, kcat)

    def consume(s, j, keep):
        v2 = v_ref[0, key_rows(j), :]
        for h in range(2):
            m_new, alpha, p = _online_softmax(s[h * tq:(h + 1) * tq], keep, m_sc[h])
            pv = jnp.dot(p, _with_ones(v2, lane, h), preferred_element_type=F32)
            acc_sc[h] = alpha * acc_sc[h] + pv
            m_sc[h] = m_new

    _causal_sweep(i, scores, consume, tq, group=4, scores_first=True)

    o_ref[0] = jnp.where(lane < HEAD_DIM, _normalize(acc_sc[0]), _normalize(acc_sc[1])).astype(o_ref.dtype)


def _fox_attention(qc, kc, vc, q_aug, k_aug):
    bn, t, _ = qc.shape
    tq = TQ
    qblk = pl.BlockSpec((1, tq, LANES), lambda b, p, i: (b, i, p))
    full = pl.BlockSpec((1, t, LANES), lambda b, p, i: (b, 0, p))
    return pl.pallas_call(
        functools.partial(_fox_kernel, tq=tq),
        grid=(bn, C_HEADS // 2, t // tq),
        in_specs=[qblk, qblk, full, full, full],
        out_specs=qblk,
        out_shape=jax.ShapeDtypeStruct((bn, t, C_W), BF16),
        scratch_shapes=[pltpu.VMEM((2, tq, LANES), F32)] * 2,
        compiler_params=_cparams("parallel", "parallel", "arbitrary"),
        name="forgetting_attention",
    )(qc, q_aug, kc, k_aug, vc)


def _diff_kernel(q_ref, k_ref, v_ref, lp_ref, g_ref, o_ref, m_sc, acc_sc, *, tq, lam_init):
    i = pl.program_id(2)
    lane = lax.broadcasted_iota(jnp.int32, (tq, LANES), 1)
    q2 = q_ref[0]
    zero = jnp.zeros_like(q2)
    nmap = LANES // DIFF_QK_DIM
    qcat = jnp.concatenate(
        [jnp.where(_lane_range(lane, g * DIFF_QK_DIM, (g + 1) * DIFF_QK_DIM), q2, zero)
         for g in range(nmap)], axis=0)
    m_sc[...] = jnp.full(m_sc.shape, NEG, F32)
    acc_sc[...] = jnp.zeros(acc_sc.shape, F32)

    def key_rows(j):
        return pl.ds(pl.multiple_of(j * tq, tq), tq)

    def scores(j):
        return _nt_dot(qcat, k_ref[0, key_rows(j), :])

    def consume(s, j, keep):
        v2 = v_ref[0, key_rows(j), :]
        for h in range(2):
            stats = [_online_softmax(s[g * tq:(g + 1) * tq], keep, m_sc[g]) for g in (2 * h, 2 * h + 1)]
            p = jnp.concatenate([st[2] for st in stats], axis=0)
            pv = jnp.dot(p, _with_ones(v2, lane, h), preferred_element_type=F32)
            for n, g in enumerate((2 * h, 2 * h + 1)):
                acc_sc[g] = stats[n][1] * acc_sc[g] + pv[n * tq:(n + 1) * tq]
                m_sc[g] = stats[n][0]

    _causal_sweep(i, scores, consume, tq, group=4, scores_first=False)

    lp = lp_ref[...]
    lam = (jnp.exp(jnp.sum(lp[0:1] * lp[1:2], axis=-1, keepdims=True))
           - jnp.exp(jnp.sum(lp[2:3] * lp[3:4], axis=-1, keepdims=True)) + lam_init)
    norm = [_normalize(acc_sc[g]) for g in range(nmap)]
    first = lane < HEAD_DIM
    o = jnp.where(first, norm[0] - lam * norm[1], norm[2] - lam * norm[3])
    osq = o * o
    ms0 = jnp.sum(jnp.where(first, osq, 0.0), axis=-1, keepdims=True) * (1.0 / HEAD_DIM)
    ms1 = jnp.sum(jnp.where(first, 0.0, osq), axis=-1, keepdims=True) * (1.0 / HEAD_DIM)
    o = o * lax.rsqrt(jnp.where(first, ms0, ms1) + SUBLN_EPS)
    o_ref[0] = (o * g_ref[...] * (1.0 - lam_init)).astype(o_ref.dtype)


def _diff_attention(qa, ka, va, lam_params, subln_g, lam_init):
    bn, t, _ = qa.shape
    tq = TQ
    g2 = jnp.concatenate([subln_g, subln_g]).reshape(1, LANES).astype(F32)
    qblk = pl.BlockSpec((1, tq, LANES), lambda b, p, i: (b, i, p))
    full = pl.BlockSpec((1, t, LANES), lambda b, p, i: (b, 0, p))
    return pl.pallas_call(
        functools.partial(_diff_kernel, tq=tq, lam_init=lam_init),
        grid=(bn, A_HEADS // 2, t // tq),
        in_specs=[qblk, full, full,
                  pl.BlockSpec((4, DIFF_QK_DIM), lambda b, p, i: (0, 0)),
                  pl.BlockSpec((1, LANES), lambda b, p, i: (0, 0))],
        out_specs=qblk,
        out_shape=jax.ShapeDtypeStruct((bn, t, A_W), BF16),
        scratch_shapes=[pltpu.VMEM((4, tq, LANES), F32)] * 2,
        compiler_params=_cparams("parallel", "parallel", "arbitrary"),
        name="diff_attention",
    )(qa, ka, va, lam_params.astype(F32), g2)


def _class_runs(u, r, d):
    g = NCLS // d
    run = BAND // g
    return [((r + d * b) * BAND + run * u, run) for b in range(g)]


def _dilated_kernel(q_ref, k_ref, v_ref, o_ref, qcm, kcm, vcm, m_sc, acc_sc, *, tile):
    i = pl.program_id(2)
    cur = i & 1
    prev = 1 - cur
    assert tile == NCLS * BAND

    @pl.when(i == 0)
    def _():
        kcm[1] = jnp.zeros((tile, LANES), F32)
        vcm[1] = jnp.zeros((tile, LANES), F32)

    for c in range(NCLS):
        dst = slice(c * BAND, (c + 1) * BAND)
        src = pl.ds(c, BAND, stride=NCLS)
        qcm[dst, :] = q_ref[0, src, :]
        kcm[cur, dst, :] = k_ref[0, src, :]
        vcm[cur, dst, :] = v_ref[0, src, :]

    row = lax.broadcasted_iota(jnp.int32, (BAND, LANES), 0)
    lane = lax.broadcasted_iota(jnp.int32, (BAND, LANES), 1)
    lane2 = lax.broadcasted_iota(jnp.int32, (2 * BAND, LANES), 1)

    def run_rows(start, n):
        return pl.ds(pl.multiple_of(start, 8), n)

    def gather(ref, slot, runs):
        pieces = [ref[run_rows(s, n), :] if slot is None else ref[slot, run_rows(s, n), :] for s, n in runs]
        return pieces[0] if len(pieces) == 1 else jnp.concatenate(pieces, axis=0)

    bodies = []
    for pat, (window, d) in enumerate(DILATED_PATTERNS):
        assert window == BAND * d and NCLS % d == 0
        g = NCLS // d
        run = BAND // g
        nq = g * (row % run) + row // run
        nk = g * (lane % run) + lane // run
        bias_cur = jnp.where(nk <= nq, 0.0, NEG)
        bias_prev = jnp.where(nk >= nq, 0.0, NEG)
        bias_first = jnp.where(i > 0, bias_prev, NEG)

        def body(idx, d=d, g=g, pat=pat, bias_cur=bias_cur, bias_prev=bias_prev, bias_first=bias_first):
            u = idx // d
            r = idx - u * d
            first = u == 0
            runs = _class_runs(u, r, d)
            pruns = _class_runs(jnp.where(first, g - 1, u - 1), r, d)
            pslot = jnp.where(first, prev, cur)
            q = gather(qcm, None, runs).astype(BF16)
            zero = jnp.zeros_like(q)
            qcat = jnp.concatenate([jnp.where(_own_half(lane, h), q, zero) for h in range(2)], axis=0)
            kcat = jnp.concatenate([gather(kcm, pslot, pruns), gather(kcm, cur, runs)], axis=0).astype(BF16)
            vcat = jnp.concatenate([gather(vcm, pslot, pruns), gather(vcm, cur, runs)], axis=0).astype(BF16)
            bias = jnp.concatenate([jnp.where(first, bias_first, bias_prev), bias_cur], axis=1)
            s = _nt_dot(qcat, kcat)
            for h in range(2):
                sh = s[h * BAND:(h + 1) * BAND] + bias
                m = jnp.broadcast_to(jnp.max(sh, axis=-1, keepdims=True), (BAND, LANES))
                p = jnp.concatenate([jnp.exp2(sb - m).astype(BF16) for sb in _lane_blocks(sh)], axis=1)
                pv = jnp.dot(p, _with_ones(vcat, lane2, h), preferred_element_type=F32)
                off = 0
                for start, n in runs:
                    acc_sc[2 * pat + h, run_rows(start, n), :] = pv[off:off + n]
                    m_sc[2 * pat + h, run_rows(start, n), :] = m[off:off + n]
                    off += n

        bodies.append(body)

    def group(it, carry):
        for k in range(DIL_UNROLL):
            for body in bodies:
                body(it * DIL_UNROLL + k)
        return carry

    lax.fori_loop(0, NCLS // DIL_UNROLL, group, 0)

    lane_t = lax.broadcasted_iota(jnp.int32, (tile, LANES), 1)
    npat = len(DILATED_PATTERNS)
    outs = []
    for h in range(2):
        ms = [m_sc[2 * pat + h] for pat in range(npat)]
        m_all = functools.reduce(jnp.maximum, ms)
        tot = sum(jnp.exp2(ms[pat] - m_all) * acc_sc[2 * pat + h] for pat in range(npat))
        outs.append(_normalize(tot))
    res = jnp.where(lane_t < HEAD_DIM, outs[0], outs[1])
    for c in range(NCLS):
        o_ref[0, pl.ds(c, BAND, stride=NCLS), :] = res[c * BAND:(c + 1) * BAND]


def _dilated_attention(qb, kb, vb):
    bn, t, _ = qb.shape
    tile = DIL_TILE
    nstate = 2 * len(DILATED_PATTERNS)
    blk = pl.BlockSpec((1, tile, LANES), lambda b, p, i: (b, i, p))
    return pl.pallas_call(
        functools.partial(_dilated_kernel, tile=tile),
        grid=(bn, B_HEADS // 2, t // tile),
        in_specs=[blk, blk, blk],
        out_specs=blk,
        out_shape=jax.ShapeDtypeStruct((bn, t, B_W), F32),
        scratch_shapes=[pltpu.VMEM((tile, LANES), F32), pltpu.VMEM((2, tile, LANES), F32),
                        pltpu.VMEM((2, tile, LANES), F32)] + [pltpu.VMEM((nstate, tile, LANES), F32)] * 2,
        compiler_params=_cparams("parallel", "parallel", "arbitrary"),
        name="dilated_attention",
    )(qb, kb, vb)


def _outproj_kernel(oa_ref, ob_ref, oc_ref, x_ref, w_ref, gate_ref, g_ref, sc_ref, sh_ref, x1_ref, h2_ref):
    o = jnp.concatenate([oa_ref[0], ob_ref[0].astype(BF16), oc_ref[0]], axis=-1)
    x1 = x_ref[0] + gate_ref[0] * jnp.dot(o, w_ref[...], preferred_element_type=F32)
    x1_ref[0] = x1
    h2_ref[0] = _modulated_norm(x1, g_ref[...], sc_ref[0], sh_ref[0]).astype(BF16)


def _out_projection(oa, ob, oc, x, w_out, gate1, g_mlp, scale2, shift2):
    bn, t, d = x.shape
    tok = lambda w: pl.BlockSpec((1, TM, w), lambda b, i: (b, i, 0))
    per_b = pl.BlockSpec((1, 1, d), lambda b, i: (b, 0, 0))
    return pl.pallas_call(
        _outproj_kernel,
        grid=(bn, t // TM),
        in_specs=[tok(A_W), tok(B_W), tok(C_W), tok(d),
                  pl.BlockSpec((d, d), lambda b, i: (0, 0)), per_b,
                  pl.BlockSpec((1, d), lambda b, i: (0, 0)), per_b, per_b],
        out_specs=[tok(d), tok(d)],
        out_shape=[jax.ShapeDtypeStruct((bn, t, d), F32), jax.ShapeDtypeStruct((bn, t, d), BF16)],
        compiler_params=_cparams("parallel", "parallel"),
        name="out_projection",
    )(oa, ob, oc, x, w_out, gate1, g_mlp.reshape(1, d), scale2, shift2)


FF_CHUNK = 256
HALO = 8


def _up_kernel(h_ref, w_ref, cw_ref, cb_ref, y_ref, halo_sc, g_sc):
    i = pl.program_id(1)

    @pl.when(i == 0)
    def _():
        halo_sc[...] = jnp.zeros(halo_sc.shape, F32)

    h = h_ref[0]
    tm = h.shape[0]
    for c in range(D_FF // FF_CHUNK):
        cs = slice(c * FF_CHUNK, (c + 1) * FF_CHUNK)
        u = jnp.dot(h, w_ref[:, cs], preferred_element_type=F32)
        g = jnp.dot(h, w_ref[:, D_FF + c * FF_CHUNK:D_FF + (c + 1) * FF_CHUNK], preferred_element_type=F32)
        g_sc[0:HALO, :] = halo_sc[:, cs]
        g_sc[HALO:HALO + tm, :] = g
        halo_sc[:, cs] = g[tm - HALO:tm, :]
        conv = (cb_ref[:, cs] + g_sc[HALO - 2:HALO - 2 + tm, :] * cw_ref[0:1, cs]
                + g_sc[HALO - 1:HALO - 1 + tm, :] * cw_ref[1:2, cs] + g * cw_ref[2:3, cs])
        act = conv * (1.0 / (1.0 + jnp.exp(-conv)))
        y_ref[0, :, cs] = (act * u).astype(BF16)


def _up_projection(h2, w_up, conv_w, conv_b):
    bn, t, d = h2.shape
    return pl.pallas_call(
        _up_kernel,
        grid=(bn, t // TM),
        in_specs=[pl.BlockSpec((1, TM, d), lambda b, i: (b, i, 0)),
                  pl.BlockSpec((d, 2 * D_FF), lambda b, i: (0, 0)),
                  pl.BlockSpec((CONV_WIDTH, D_FF), lambda b, i: (0, 0)),
                  pl.BlockSpec((1, D_FF), lambda b, i: (0, 0))],
        out_specs=pl.BlockSpec((1, TM, D_FF), lambda b, i: (b, i, 0)),
        out_shape=jax.ShapeDtypeStruct((bn, t, D_FF), BF16),
        scratch_shapes=[pltpu.VMEM((HALO, D_FF), F32), pltpu.VMEM((HALO + TM, FF_CHUNK), F32)],
        compiler_params=_cparams("parallel", "arbitrary"),
        name="up_projection",
    )(h2, w_up, conv_w, conv_b.reshape(1, D_FF))


def _down_kernel(y_ref, x_ref, w_ref, gate_ref, g_ref, o_ref, *, final_norm):
    x2 = x_ref[0] + gate_ref[0] * jnp.dot(y_ref[0], w_ref[...], preferred_element_type=F32)
    if final_norm:
        ms = jnp.mean(x2 * x2, axis=-1, keepdims=True)
        x2 = x2 * lax.rsqrt(ms + NORM_EPS) * g_ref[...]
    o_ref[0] = x2


def _down_projection(y, x1, w_down, gate2, g_final, final_norm):
    bn, t, d = x1.shape
    return pl.pallas_call(
        functools.partial(_down_kernel, final_norm=final_norm),
        grid=(bn, t // TM),
        in_specs=[pl.BlockSpec((1, TM, D_FF), lambda b, i: (b, i, 0)),
                  pl.BlockSpec((1, TM, d), lambda b, i: (b, i, 0)),
                  pl.BlockSpec((D_FF, d), lambda b, i: (0, 0)),
                  pl.BlockSpec((1, 1, d), lambda b, i: (b, 0, 0)),
                  pl.BlockSpec((1, d), lambda b, i: (0, 0))],
        out_specs=pl.BlockSpec((1, TM, d), lambda b, i: (b, i, 0)),
        out_shape=jax.ShapeDtypeStruct((bn, t, d), F32),
        compiler_params=_cparams("parallel", "parallel"),
        name="down_projection",
    )(y, x1, w_down, gate2, g_final.reshape(1, d))


def _rope_table(t, dim):
    inv = 1.0 / (ROPE_THETA ** (jnp.arange(0, dim, 2, dtype=F32) / dim))
    ang = jnp.arange(t, dtype=F32)[:, None] * inv[None, :]
    cos, sin = jnp.cos(ang), jnp.sin(ang)
    reps = LANES // dim
    return (jnp.tile(jnp.concatenate([cos, cos], axis=-1), (1, reps)),
            jnp.tile(jnp.concatenate([-sin, sin], axis=-1), (1, reps)))


def kernel(x, c, w_mod, b_mod, g_attn, w_in, diff_lambda, subln_g, forget_bias, w_out,
           g_mlp, w_up, conv_w, conv_b, w_down, g_final):
    depth = w_mod.shape[0]
    bn, t, d = x.shape
    assert d == D_MODEL and t % DIL_TILE == 0 and t % TM == 0 and t % TQ == 0
    tabs = _rope_table(t, DIFF_QK_DIM) + _rope_table(t, HEAD_DIM)
    mod = _modulation(c, w_mod, b_mod)
    w_in_p = jnp.pad(w_in, ((0, 0), (0, 0), (0, IN_COLS_PAD - IN_COLS))).astype(BF16)
    w_out_b, w_up_b, w_down_b = w_out.astype(BF16), w_up.astype(BF16), w_down.astype(BF16)

    for layer in range(depth):
        lam_init = 0.8 - 0.6 * math.exp(-0.3 * layer)
        shift1, scale1, gate1, shift2, scale2, gate2 = [
            m[:, None, :] for m in jnp.split(mod[layer], 6, axis=-1)]
        qa, ka, va, qb, kb, vb, qc, kc, vc, fz = _in_projection(
            x, g_attn[layer], scale1, shift1, w_in_p[layer], tabs)
        q_aug, k_aug = _bias_operands(_cum_logf(fz, forget_bias[layer]))
        oa = _diff_attention(qa, ka, va, diff_lambda[layer], subln_g[layer], lam_init)
        ob = _dilated_attention(qb, kb, vb)
        oc = _fox_attention(qc, kc, vc, q_aug, k_aug)
        x1, h2 = _out_projection(oa, ob, oc, x, w_out_b[layer], gate1, g_mlp[layer], scale2, shift2)
        y = _up_projection(h2, w_up_b[layer], conv_w[layer], conv_b[layer])
        x = _down_projection(y, x1, w_down_b[layer], gate2, g_final, layer == depth - 1)
    return x
```

```python
import functools
import math
from typing import Any, NamedTuple

import jax
import jax.numpy as jnp
from jax import lax
from jax.experimental import pallas as pl
from jax.experimental.pallas import tpu as pltpu

F32 = jnp.float32
BF16 = jnp.bfloat16

D_MODEL = 1024
HEAD_DIM = 64
A_HEADS = 4
DIFF_QK_DIM = 32
B_HEADS = 6
C_HEADS = 6
DILATED_PATTERNS = ((128, 1), (512, 4), (2048, 16))
ROPE_THETA = 10000.0
D_FF = 2816
CONV_WIDTH = 3
NORM_EPS = 1e-6
SUBLN_EPS = 1e-5

A_W = A_HEADS * HEAD_DIM
B_W = B_HEADS * HEAD_DIM
C_W = C_HEADS * HEAD_DIM
IN_COLS = 3 * A_W + 3 * B_W + 3 * C_W + C_HEADS
LANES = 128
IN_COLS_PAD = 3 * A_W + 3 * B_W + 3 * C_W + LANES

LOG2E = 1.4426950408889634
NEG = -1e30
VMEM_LIMIT = 56 * 2**20
BIAS_PIECES = 3

TM = 512
TQ = 512
DIL_TILE = 2048
BAND = 128
NCLS = DIL_TILE // BAND
DIL_UNROLL = 8


def _cparams(*sem):
    return pltpu.CompilerParams(dimension_semantics=sem, vmem_limit_bytes=VMEM_LIMIT)


def _nt_dot(a, b):
    return lax.dot_general(a, b, (((1,), (1,)), ((), ())), preferred_element_type=F32)


def _mod_kernel(c_ref, w_ref, b_ref, o_ref):
    c = c_ref[...]
    sc = c * (1.0 / (1.0 + jnp.exp(-c)))
    o_ref[0] = jnp.dot(sc, w_ref[0], preferred_element_type=F32,
                       precision=lax.Precision.HIGHEST) + b_ref[0]


def _modulation(c, w_mod, b_mod):
    depth, d, n = w_mod.shape
    bn = c.shape[0]
    nb = n // d
    return pl.pallas_call(
        _mod_kernel,
        grid=(depth, nb),
        in_specs=[
            pl.BlockSpec((bn, d), lambda l, j: (0, 0)),
            pl.BlockSpec((1, d, d), lambda l, j: (l, 0, j)),
            pl.BlockSpec((1, 1, d), lambda l, j: (l, 0, j)),
        ],
        out_specs=pl.BlockSpec((1, bn, d), lambda l, j: (l, 0, j)),
        out_shape=jax.ShapeDtypeStruct((depth, bn, n), F32),
        compiler_params=_cparams("parallel", "parallel"),
        name="modulation",
    )(c, w_mod, b_mod.reshape(depth, 1, n))


def _modulated_norm(x, g, scale, shift):
    ms = jnp.mean(x * x, axis=-1, keepdims=True)
    y = x * lax.rsqrt(ms + NORM_EPS) * g
    return y * (1.0 + scale) + shift


def _rope128(r, cos, sin_signed, group):
    half = group // 2
    lane = lax.broadcasted_iota(jnp.int32, r.shape, 1)
    first = (lane % group) < half
    rot = jnp.where(first, pltpu.roll(r, LANES - half, 1), pltpu.roll(r, half, 1))
    return r * cos + rot * sin_signed


def _inproj_kernel(x_ref, g_ref, sc_ref, sh_ref, w_ref, cosa_ref, sina_ref, cosb_ref, sinb_ref,
                   qa_ref, ka_ref, va_ref, qb_ref, kb_ref, vb_ref, qc_ref, kc_ref, vc_ref, fz_ref):
    h = _modulated_norm(x_ref[0], g_ref[...], sc_ref[0], sh_ref[0]).astype(BF16)
    cosa, sina = cosa_ref[...], sina_ref[...]
    cosb, sinb = cosb_ref[...], sinb_ref[...]
    qa_scale = DIFF_QK_DIM ** -0.5 * LOG2E
    qb_scale = HEAD_DIM ** -0.5 * LOG2E
    qc_scale = HEAD_DIM ** -0.5 * LOG2E

    ra = jnp.dot(h, w_ref[:, 0:3 * A_W], preferred_element_type=F32)
    for blk in range(A_W // LANES):
        sl = slice(blk * LANES, (blk + 1) * LANES)
        q = ra[:, blk * LANES:(blk + 1) * LANES]
        k = ra[:, A_W + blk * LANES:A_W + (blk + 1) * LANES]
        qa_ref[0, :, sl] = (_rope128(q, cosa, sina, DIFF_QK_DIM) * qa_scale).astype(BF16)
        ka_ref[0, :, sl] = _rope128(k, cosa, sina, DIFF_QK_DIM).astype(BF16)
    va_ref[0] = ra[:, 2 * A_W:3 * A_W].astype(BF16)

    off = 3 * A_W
    rb = jnp.dot(h, w_ref[:, off:off + 3 * B_W], preferred_element_type=F32)
    for blk in range(B_W // LANES):
        sl = slice(blk * LANES, (blk + 1) * LANES)
        q = rb[:, blk * LANES:(blk + 1) * LANES]
        k = rb[:, B_W + blk * LANES:B_W + (blk + 1) * LANES]
        qb_ref[0, :, sl] = _rope128(q, cosb, sinb, HEAD_DIM) * qb_scale
        kb_ref[0, :, sl] = _rope128(k, cosb, sinb, HEAD_DIM)
    vb_ref[0] = rb[:, 2 * B_W:3 * B_W]

    off = 3 * A_W + 3 * B_W
    rc = jnp.dot(h, w_ref[:, off:off + 3 * C_W + LANES], preferred_element_type=F32)
    qc_ref[0] = (rc[:, 0:C_W] * qc_scale).astype(BF16)
    kc_ref[0] = rc[:, C_W:2 * C_W].astype(BF16)
    vc_ref[0] = rc[:, 2 * C_W:3 * C_W].astype(BF16)
    fz_ref[0] = rc[:, 3 * C_W:3 * C_W + LANES]


def _in_projection(x, g, scale, shift, w_in_p, tabs):
    bn, t, d = x.shape
    cosa, sina, cosb, sinb = tabs
    tok = lambda w: pl.BlockSpec((1, TM, w), lambda b, i: (b, i, 0))
    per_b = pl.BlockSpec((1, 1, d), lambda b, i: (b, 0, 0))
    tab = pl.BlockSpec((TM, LANES), lambda b, i: (i, 0))
    widths = [A_W] * 3 + [B_W] * 3 + [C_W] * 3 + [LANES]
    dtypes = [BF16] * 3 + [F32] * 3 + [BF16] * 3 + [F32]
    return pl.pallas_call(
        _inproj_kernel,
        grid=(bn, t // TM),
        in_specs=[tok(d), pl.BlockSpec((1, d), lambda b, i: (0, 0)), per_b, per_b,
                  pl.BlockSpec((d, IN_COLS_PAD), lambda b, i: (0, 0)), tab, tab, tab, tab],
        out_specs=[tok(w) for w in widths],
        out_shape=[jax.ShapeDtypeStruct((bn, t, w), dt) for w, dt in zip(widths, dtypes)],
        compiler_params=_cparams("parallel", "parallel"),
        name="in_projection",
    )(x, g.reshape(1, d), scale, shift, w_in_p, cosa, sina, cosb, sinb)


def _cumlogf_kernel(z_ref, b_ref, u_ref, lb_ref, o_ref):
    z = z_ref[0] + b_ref[...]
    logf = jnp.minimum(z, 0.0) - jnp.log1p(jnp.exp(-jnp.abs(z)))
    hi = lax.Precision.HIGHEST
    within = jnp.dot(logf, u_ref[...], preferred_element_type=F32, precision=hi)
    before = jnp.dot(lb_ref[...], within, preferred_element_type=F32, precision=hi)
    cum = (within + before[:, LANES - 1:LANES]) * LOG2E
    rest = cum
    for piece in range(BIAS_PIECES):
        part = rest.astype(BF16).astype(F32)
        o_ref[0, piece] = part
        rest = rest - part


def _cum_logf(fz, forget_bias):
    bn, t, _ = fz.shape
    nc = t // LANES
    rows = 8 * nc
    z = jnp.transpose(fz[:, :, :8], (0, 2, 1)).reshape(bn, rows, LANES)
    bias = jnp.concatenate([forget_bias.astype(F32), jnp.zeros((8 - C_HEADS,), F32)])
    bias = jnp.broadcast_to(jnp.repeat(bias, nc)[:, None], (rows, LANES))
    idx = jnp.arange(LANES)
    upper = (idx[:, None] <= idx[None, :]).astype(F32)
    r = jnp.arange(rows)
    lblk = ((r[:, None] // nc == r[None, :] // nc) & (r[None, :] < r[:, None])).astype(F32)
    cum = pl.pallas_call(
        _cumlogf_kernel,
        grid=(bn,),
        in_specs=[pl.BlockSpec((1, rows, LANES), lambda b: (b, 0, 0)),
                  pl.BlockSpec((rows, LANES), lambda b: (0, 0)),
                  pl.BlockSpec((LANES, LANES), lambda b: (0, 0)),
                  pl.BlockSpec((rows, rows), lambda b: (0, 0))],
        out_specs=pl.BlockSpec((1, BIAS_PIECES, rows, LANES), lambda b: (b, 0, 0, 0)),
        out_shape=jax.ShapeDtypeStruct((bn, BIAS_PIECES, rows, LANES), F32),
        compiler_params=_cparams("parallel"),
        name="cum_logf",
    )(z, bias, upper, lblk)
    return cum.reshape(bn, BIAS_PIECES, 8, t)


def _bias_operands(cum):
    bn, _, _, t = cum.shape
    pieces = jnp.transpose(cum[:, :, :C_HEADS], (0, 2, 1, 3)).astype(BF16)
    ones = jnp.ones_like(pieces)
    pad = jnp.zeros((bn, C_HEADS, 8 - 2 * BIAS_PIECES, t), BF16)

    def pack(first, second):
        blk = jnp.concatenate([first, second, pad], axis=2).reshape(bn, C_HEADS // 2, 16, t)
        blk = jnp.pad(blk, ((0, 0), (0, 0), (0, LANES - 16), (0, 0)))
        return jnp.transpose(blk, (0, 3, 1, 2)).reshape(bn, t, C_W)

    return pack(ones, pieces), pack(-pieces, ones)


def _lane_blocks(s):
    return [s[:, cb * LANES:(cb + 1) * LANES] for cb in range(s.shape[1] // LANES)]


def _lane_range(lane, lo, hi):
    return (lane >= lo) & (lane < hi)


def _own_half(lane, h):
    return lane < HEAD_DIM if h == 0 else lane >= HEAD_DIM


def _with_ones(v2, h):
    lane = lax.broadcasted_iota(jnp.int32, v2.shape, 1)
    return jnp.where(_own_half(lane, h), v2, jnp.ones_like(v2))


def _online_softmax(s, keep, m_prev):
    blocks = _lane_blocks(s)
    if keep is not None:
        blocks = [sb if kb is None else jnp.where(kb, sb, NEG) for kb, sb in zip(keep, blocks)]
    mpart = blocks[0]
    for sb in blocks[1:]:
        mpart = jnp.maximum(mpart, sb)
    m_new = jnp.maximum(m_prev, jnp.max(mpart, axis=-1, keepdims=True))
    alpha = jnp.exp2(m_prev - m_new)
    p = jnp.concatenate([jnp.exp2(sb - m_new).astype(BF16) for sb in blocks], axis=1)
    return m_new, alpha, p


class _Part(NamedTuple):
    r0: int
    nrows: int
    nkeys: int
    keep: Any


def _diagonal_parts(tq):
    half = tq // 2
    row = lax.broadcasted_iota(jnp.int32, (half, LANES), 0)
    lane = lax.broadcasted_iota(jnp.int32, (half, LANES), 1)
    tri = [lane + cb * LANES <= row for cb in range(half // LANES)]
    return [_Part(0, half, half, tri), _Part(half, half, tq, [None] * (half // LANES) + tri)]


def _causal_sweep(i, scores, consume, tq, group, scores_first):
    full = _Part(0, tq, tq, None)

    def run(work):
        if scores_first:
            tiles = [scores(j, part) for j, part in work]
            for s, (j, part) in zip(tiles, work):
                consume(s, j, part)
        else:
            for j, part in work:
                consume(scores(j, part), j, part)

    def body(p, carry):
        run([(group * p + k, full) for k in range(group)])
        return carry

    shift = group.bit_length() - 1
    assert group == 1 << shift and group >= 2
    lax.fori_loop(0, lax.shift_right_logical(i, shift), body, 0)
    half = group // 2
    while half >= 2:
        done = i & ~(2 * half - 1)

        @pl.when((i & half) != 0)
        def _(done=done, half=half):
            run([(done + k, full) for k in range(half)])

        half //= 2

    diagonal = [(i, part) for part in _diagonal_parts(tq)]
    odd = (i & 1) != 0

    @pl.when(odd)
    def _():
        run([(i - 1, full)] + diagonal)

    @pl.when(jnp.logical_not(odd))
    def _():
        run(diagonal)


def _stack_rows(qcat, nstack, tq, part):
    if part.nrows == tq:
        return qcat
    return jnp.concatenate([qcat[g * tq + part.r0:g * tq + part.r0 + part.nrows] for g in range(nstack)], axis=0)


def _normalize(acc):
    return acc / pltpu.roll(acc, HEAD_DIM, 1)


def _fox_kernel(q_ref, qx_ref, k_ref, kx_ref, v_ref, o_ref, m_sc, acc_sc, *, tq):
    i = pl.program_id(2)
    lane = lax.broadcasted_iota(jnp.int32, (tq, LANES), 1)
    q2, qx = q_ref[0], qx_ref[0]
    zero = jnp.zeros_like(q2)
    qcat = jnp.concatenate(
        [jnp.concatenate([jnp.where(_own_half(lane, h), q2, zero),
                          jnp.where(_lane_range(lane, 8 * h, 8 * h + 8), qx, zero)], axis=1)
         for h in range(2)], axis=0)
    m_sc[...] = jnp.full(m_sc.shape, NEG, F32)
    acc_sc[...] = jnp.zeros(acc_sc.shape, F32)

    def key_rows(j, n):
        return pl.ds(pl.multiple_of(j * tq, tq), n)

    def scores(j, part):
        keys = key_rows(j, part.nkeys)
        kcat = jnp.concatenate([k_ref[0, keys, :], kx_ref[0, keys, :]], axis=1)
        return _nt_dot(_stack_rows(qcat, 2, tq, part), kcat)

    def consume(s, j, part):
        rows = slice(part.r0, part.r0 + part.nrows)
        v2 = v_ref[0, key_rows(j, part.nkeys), :]
        for h in range(2):
            m_new, alpha, p = _online_softmax(s[h * part.nrows:(h + 1) * part.nrows], part.keep, m_sc[h, rows])
            pv = jnp.dot(p, _with_ones(v2, h), preferred_element_type=F32)
            acc_sc[h, rows] = alpha * acc_sc[h, rows] + pv
            m_sc[h, rows] = m_new

    _causal_sweep(i, scores, consume, tq, group=4, scores_first=True)

    o_ref[0] = jnp.where(lane < HEAD_DIM, _normalize(acc_sc[0]), _normalize(acc_sc[1])).astype(o_ref.dtype)


def _fox_attention(qc, kc, vc, q_aug, k_aug):
    bn, t, _ = qc.shape
    tq = TQ
    qblk = pl.BlockSpec((1, tq, LANES), lambda b, p, i: (b, i, p))
    full = pl.BlockSpec((1, t, LANES), lambda b, p, i: (b, 0, p))
    return pl.pallas_call(
        functools.partial(_fox_kernel, tq=tq),
        grid=(bn, C_HEADS // 2, t // tq),
        in_specs=[qblk, qblk, full, full, full],
        out_specs=qblk,
        out_shape=jax.ShapeDtypeStruct((bn, t, C_W), BF16),
        scratch_shapes=[pltpu.VMEM((2, tq, LANES), F32)] * 2,
        compiler_params=_cparams("parallel", "parallel", "arbitrary"),
        name="forgetting_attention",
    )(qc, q_aug, kc, k_aug, vc)


def _diff_kernel(q_ref, k_ref, v_ref, lp_ref, g_ref, o_ref, m_sc, acc_sc, *, tq, lam_init):
    i = pl.program_id(2)
    lane = lax.broadcasted_iota(jnp.int32, (tq, LANES), 1)
    q2 = q_ref[0]
    zero = jnp.zeros_like(q2)
    nmap = LANES // DIFF_QK_DIM
    qcat = jnp.concatenate(
        [jnp.where(_lane_range(lane, g * DIFF_QK_DIM, (g + 1) * DIFF_QK_DIM), q2, zero)
         for g in range(nmap)], axis=0)
    m_sc[...] = jnp.full(m_sc.shape, NEG, F32)
    acc_sc[...] = jnp.zeros(acc_sc.shape, F32)

    def key_rows(j, n):
        return pl.ds(pl.multiple_of(j * tq, tq), n)

    def scores(j, part):
        return _nt_dot(_stack_rows(qcat, nmap, tq, part), k_ref[0, key_rows(j, part.nkeys), :])

    def consume(s, j, part):
        rows = slice(part.r0, part.r0 + part.nrows)
        nr = part.nrows
        v2 = v_ref[0, key_rows(j, part.nkeys), :]
        for h in range(2):
            stats = [_online_softmax(s[g * nr:(g + 1) * nr], part.keep, m_sc[g, rows]) for g in (2 * h, 2 * h + 1)]
            p = jnp.concatenate([st[2] for st in stats], axis=0)
            pv = jnp.dot(p, _with_ones(v2, h), preferred_element_type=F32)
            for n, g in enumerate((2 * h, 2 * h + 1)):
                acc_sc[g, rows] = stats[n][1] * acc_sc[g, rows] + pv[n * nr:(n + 1) * nr]
                m_sc[g, rows] = stats[n][0]

    _causal_sweep(i, scores, consume, tq, group=4, scores_first=False)

    lp = lp_ref[...]
    lam = (jnp.exp(jnp.sum(lp[0:1] * lp[1:2], axis=-1, keepdims=True))
           - jnp.exp(jnp.sum(lp[2:3] * lp[3:4], axis=-1, keepdims=True)) + lam_init)
    norm = [_normalize(acc_sc[g]) for g in range(nmap)]
    first = lane < HEAD_DIM
    o = jnp.where(first, norm[0] - lam * norm[1], norm[2] - lam * norm[3])
    osq = o * o
    ms0 = jnp.sum(jnp.where(first, osq, 0.0), axis=-1, keepdims=True) * (1.0 / HEAD_DIM)
    ms1 = jnp.sum(jnp.where(first, 0.0, osq), axis=-1, keepdims=True) * (1.0 / HEAD_DIM)
    o = o * lax.rsqrt(jnp.where(first, ms0, ms1) + SUBLN_EPS)
    o_ref[0] = (o * g_ref[...] * (1.0 - lam_init)).astype(o_ref.dtype)


def _diff_attention(qa, ka, va, lam_params, subln_g, lam_init):
    bn, t, _ = qa.shape
    tq = TQ
    g2 = jnp.concatenate([subln_g, subln_g]).reshape(1, LANES).astype(F32)
    qblk = pl.BlockSpec((1, tq, LANES), lambda b, p, i: (b, i, p))
    full = pl.BlockSpec((1, t, LANES), lambda b, p, i: (b, 0, p))
    return pl.pallas_call(
        functools.partial(_diff_kernel, tq=tq, lam_init=lam_init),
        grid=(bn, A_HEADS // 2, t // tq),
        in_specs=[qblk, full, full,
                  pl.BlockSpec((4, DIFF_QK_DIM), lambda b, p, i: (0, 0)),
                  pl.BlockSpec((1, LANES), lambda b, p, i: (0, 0))],
        out_specs=qblk,
        out_shape=jax.ShapeDtypeStruct((bn, t, A_W), BF16),
        scratch_shapes=[pltpu.VMEM((4, tq, LANES), F32)] * 2,
        compiler_params=_cparams("parallel", "parallel", "arbitrary"),
        name="diff_attention",
    )(qa, ka, va, lam_params.astype(F32), g2)


def _class_runs(u, r, d):
    g = NCLS // d
    run = BAND // g
    return [((r + d * b) * BAND + run * u, run) for b in range(g)]


def _dilated_kernel(q_ref, k_ref, v_ref, o_ref, qcm, kcm, vcm, m_sc, acc_sc, *, tile):
    i = pl.program_id(2)
    cur = i & 1
    prev = 1 - cur
    assert tile == NCLS * BAND

    @pl.when(i == 0)
    def _():
        kcm[1] = jnp.zeros((tile, LANES), F32)
        vcm[1] = jnp.zeros((tile, LANES), F32)

    for c in range(NCLS):
        dst = slice(c * BAND, (c + 1) * BAND)
        src = pl.ds(c, BAND, stride=NCLS)
        qcm[dst, :] = q_ref[0, src, :]
        kcm[cur, dst, :] = k_ref[0, src, :]
        vcm[cur, dst, :] = v_ref[0, src, :]

    row = lax.broadcasted_iota(jnp.int32, (BAND, LANES), 0)
    lane = lax.broadcasted_iota(jnp.int32, (BAND, LANES), 1)

    def run_rows(start, n):
        return pl.ds(pl.multiple_of(start, 8), n)

    def gather(ref, slot, runs):
        pieces = [ref[run_rows(s, n), :] if slot is None else ref[slot, run_rows(s, n), :] for s, n in runs]
        return pieces[0] if len(pieces) == 1 else jnp.concatenate(pieces, axis=0)

    bodies = []
    for pat, (window, d) in enumerate(DILATED_PATTERNS):
        assert window == BAND * d and NCLS % d == 0
        g = NCLS // d
        run = BAND // g
        nq = g * (row % run) + row // run
        nk = g * (lane % run) + lane // run
        bias_cur = jnp.where(nk <= nq, 0.0, NEG)
        bias_prev = jnp.where(nk >= nq, 0.0, NEG)
        bias_first = jnp.where(i > 0, bias_prev, NEG)

        def body(idx, d=d, g=g, pat=pat, bias_cur=bias_cur, bias_prev=bias_prev, bias_first=bias_first):
            u = idx // d
            r = idx - u * d
            first = u == 0
            runs = _class_runs(u, r, d)
            pruns = _class_runs(jnp.where(first, g - 1, u - 1), r, d)
            pslot = jnp.where(first, prev, cur)
            q = gather(qcm, None, runs).astype(BF16)
            zero = jnp.zeros_like(q)
            qcat = jnp.concatenate([jnp.where(_own_half(lane, h), q, zero) for h in range(2)], axis=0)
            kcat = jnp.concatenate([gather(kcm, pslot, pruns), gather(kcm, cur, runs)], axis=0).astype(BF16)
            vcat = jnp.concatenate([gather(vcm, pslot, pruns), gather(vcm, cur, runs)], axis=0).astype(BF16)
            bias = jnp.concatenate([jnp.where(first, bias_first, bias_prev), bias_cur], axis=1)
            s = _nt_dot(qcat, kcat)
            for h in range(2):
                sh = s[h * BAND:(h + 1) * BAND] + bias
                m = jnp.broadcast_to(jnp.max(sh, axis=-1, keepdims=True), (BAND, LANES))
                p = jnp.concatenate([jnp.exp2(sb - m).astype(BF16) for sb in _lane_blocks(sh)], axis=1)
                pv = jnp.dot(p, _with_ones(vcat, h), preferred_element_type=F32)
                off = 0
                for start, n in runs:
                    acc_sc[2 * pat + h, run_rows(start, n), :] = pv[off:off + n]
                    m_sc[2 * pat + h, run_rows(start, n), :] = m[off:off + n]
                    off += n

        bodies.append(body)

    def group(it, carry):
        for k in range(DIL_UNROLL):
            for body in bodies:
                body(it * DIL_UNROLL + k)
        return carry

    lax.fori_loop(0, NCLS // DIL_UNROLL, group, 0)

    lane_t = lax.broadcasted_iota(jnp.int32, (tile, LANES), 1)
    npat = len(DILATED_PATTERNS)
    outs = []
    for h in range(2):
        ms = [m_sc[2 * pat + h] for pat in range(npat)]
        m_all = functools.reduce(jnp.maximum, ms)
        tot = sum(jnp.exp2(ms[pat] - m_all) * acc_sc[2 * pat + h] for pat in range(npat))
        outs.append(_normalize(tot))
    res = jnp.where(lane_t < HEAD_DIM, outs[0], outs[1])
    for c in range(NCLS):
        o_ref[0, pl.ds(c, BAND, stride=NCLS), :] = res[c * BAND:(c + 1) * BAND]


def _dilated_attention(qb, kb, vb):
    bn, t, _ = qb.shape
    tile = DIL_TILE
    npat = len(DILATED_PATTERNS)
    blk = pl.BlockSpec((1, tile, LANES), lambda b, p, i: (b, i, p))
    return pl.pallas_call(
        functools.partial(_dilated_kernel, tile=tile),
        grid=(bn, B_HEADS // 2, t // tile),
        in_specs=[blk, blk, blk],
        out_specs=blk,
        out_shape=jax.ShapeDtypeStruct((bn, t, B_W), F32),
        scratch_shapes=[pltpu.VMEM((tile, LANES), F32), pltpu.VMEM((2, tile, LANES), F32),
                        pltpu.VMEM((2, tile, LANES), F32)] + [pltpu.VMEM((2 * npat, tile, LANES), F32)] * 2,
        compiler_params=_cparams("parallel", "parallel", "arbitrary"),
        name="dilated_attention",
    )(qb, kb, vb)


FF_CHUNK = 256
HALO = 8


def _resident(shape):
    return pl.BlockSpec(shape, lambda b, i: (0,) * len(shape), pipeline_mode=pl.Buffered(1))


def _mix_up_kernel(oa_ref, ob_ref, oc_ref, x_ref, wo_ref, gate_ref, g_ref, sc_ref, sh_ref,
                   w_ref, cw_ref, cb_ref, x1_ref, y_ref, halo_sc, g_sc):
    i = pl.program_id(1)

    @pl.when(i == 0)
    def _():
        halo_sc[...] = jnp.zeros(halo_sc.shape, F32)

    o = jnp.concatenate([oa_ref[0], ob_ref[0].astype(BF16), oc_ref[0]], axis=-1)
    x1 = x_ref[0] + gate_ref[0] * jnp.dot(o, wo_ref[...], preferred_element_type=F32)
    x1_ref[0] = x1
    h = _modulated_norm(x1, g_ref[...], sc_ref[0], sh_ref[0]).astype(BF16)
    tm = h.shape[0]
    for c in range(D_FF // FF_CHUNK):
        cs = slice(c * FF_CHUNK, (c + 1) * FF_CHUNK)
        u = jnp.dot(h, w_ref[:, cs], preferred_element_type=F32)
        g = jnp.dot(h, w_ref[:, D_FF + c * FF_CHUNK:D_FF + (c + 1) * FF_CHUNK], preferred_element_type=F32)
        g_sc[0:HALO, :] = halo_sc[:, cs]
        g_sc[HALO:HALO + tm, :] = g
        halo_sc[:, cs] = g[tm - HALO:tm, :]
        conv = (cb_ref[:, cs] + g_sc[HALO - 2:HALO - 2 + tm, :] * cw_ref[0:1, cs]
                + g_sc[HALO - 1:HALO - 1 + tm, :] * cw_ref[1:2, cs] + g * cw_ref[2:3, cs])
        act = conv * (1.0 / (1.0 + jnp.exp(-conv)))
        y_ref[0, :, cs] = (act * u).astype(BF16)


def _mix_up_projection(oa, ob, oc, x, w_out, gate1, g_mlp, scale2, shift2, w_up, conv_w, conv_b):
    bn, t, d = x.shape
    tok = lambda w: pl.BlockSpec((1, TM, w), lambda b, i: (b, i, 0))
    per_b = pl.BlockSpec((1, 1, d), lambda b, i: (b, 0, 0))
    return pl.pallas_call(
        _mix_up_kernel,
        grid=(bn, t // TM),
        in_specs=[tok(A_W), tok(B_W), tok(C_W), tok(d), _resident((d, d)), per_b,
                  _resident((1, d)), per_b, per_b,
                  _resident((d, 2 * D_FF)), _resident((CONV_WIDTH, D_FF)), _resident((1, D_FF))],
        out_specs=[tok(d), tok(D_FF)],
        out_shape=[jax.ShapeDtypeStruct((bn, t, d), F32), jax.ShapeDtypeStruct((bn, t, D_FF), BF16)],
        scratch_shapes=[pltpu.VMEM((HALO, D_FF), F32), pltpu.VMEM((HALO + TM, FF_CHUNK), F32)],
        compiler_params=_cparams("parallel", "arbitrary"),
        name="mix_up_projection",
    )(oa, ob, oc, x, w_out, gate1, g_mlp.reshape(1, d), scale2, shift2, w_up, conv_w, conv_b.reshape(1, D_FF))


def _down_kernel(y_ref, x_ref, w_ref, gate_ref, g_ref, o_ref, *, final_norm):
    x2 = x_ref[0] + gate_ref[0] * jnp.dot(y_ref[0], w_ref[...], preferred_element_type=F32)
    if final_norm:
        ms = jnp.mean(x2 * x2, axis=-1, keepdims=True)
        x2 = x2 * lax.rsqrt(ms + NORM_EPS) * g_ref[...]
    o_ref[0] = x2


def _down_projection(y, x1, w_down, gate2, g_final, final_norm):
    bn, t, d = x1.shape
    return pl.pallas_call(
        functools.partial(_down_kernel, final_norm=final_norm),
        grid=(bn, t // TM),
        in_specs=[pl.BlockSpec((1, TM, D_FF), lambda b, i: (b, i, 0)),
                  pl.BlockSpec((1, TM, d), lambda b, i: (b, i, 0)),
                  pl.BlockSpec((D_FF, d), lambda b, i: (0, 0)),
                  pl.BlockSpec((1, 1, d), lambda b, i: (b, 0, 0)),
                  pl.BlockSpec((1, d), lambda b, i: (0, 0))],
        out_specs=pl.BlockSpec((1, TM, d), lambda b, i: (b, i, 0)),
        out_shape=jax.ShapeDtypeStruct((bn, t, d), F32),
        compiler_params=_cparams("parallel", "parallel"),
        name="down_projection",
    )(y, x1, w_down, gate2, g_final.reshape(1, d))


def _rope_table(t, dim):
    inv = 1.0 / (ROPE_THETA ** (jnp.arange(0, dim, 2, dtype=F32) / dim))
    ang = jnp.arange(t, dtype=F32)[:, None] * inv[None, :]
    cos, sin = jnp.cos(ang), jnp.sin(ang)
    reps = LANES // dim
    return (jnp.tile(jnp.concatenate([cos, cos], axis=-1), (1, reps)),
            jnp.tile(jnp.concatenate([-sin, sin], axis=-1), (1, reps)))


def kernel(x, c, w_mod, b_mod, g_attn, w_in, diff_lambda, subln_g, forget_bias, w_out,
           g_mlp, w_up, conv_w, conv_b, w_down, g_final):
    depth = w_mod.shape[0]
    bn, t, d = x.shape
    assert d == D_MODEL and t % DIL_TILE == 0 and t % TM == 0 and t % TQ == 0
    tabs = _rope_table(t, DIFF_QK_DIM) + _rope_table(t, HEAD_DIM)
    mod = _modulation(c, w_mod, b_mod)
    w_in_p = jnp.pad(w_in, ((0, 0), (0, 0), (0, IN_COLS_PAD - IN_COLS))).astype(BF16)
    w_out_b, w_up_b, w_down_b = w_out.astype(BF16), w_up.astype(BF16), w_down.astype(BF16)

    for layer in range(depth):
        lam_init = 0.8 - 0.6 * math.exp(-0.3 * layer)
        shift1, scale1, gate1, shift2, scale2, gate2 = [
            m[:, None, :] for m in jnp.split(mod[layer], 6, axis=-1)]
        qa, ka, va, qb, kb, vb, qc, kc, vc, fz = _in_projection(
            x, g_attn[layer], scale1, shift1, w_in_p[layer], tabs)
        q_aug, k_aug = _bias_operands(_cum_logf(fz, forget_bias[layer]))
        oa = _diff_attention(qa, ka, va, diff_lambda[layer], subln_g[layer], lam_init)
        ob = _dilated_attention(qb, kb, vb)
        oc = _fox_attention(qc, kc, vc, q_aug, k_aug)
        x1, y = _mix_up_projection(oa, ob, oc, x, w_out_b[layer], gate1, g_mlp[layer], scale2, shift2,
                                   w_up_b[layer], conv_w[layer], conv_b[layer])
        x = _down_projection(y, x1, w_down_b[layer], gate2, g_final, layer == depth - 1)
    return x
```

```python
import functools
import math
from typing import Any, NamedTuple

import jax
import jax.numpy as jnp
from jax import lax
from jax.experimental import pallas as pl
from jax.experimental.pallas import tpu as pltpu

F32 = jnp.float32
BF16 = jnp.bfloat16

D_MODEL = 1024
HEAD_DIM = 64
A_HEADS = 4
DIFF_QK_DIM = 32
B_HEADS = 6
C_HEADS = 6
DILATED_PATTERNS = ((128, 1), (512, 4), (2048, 16))
ROPE_THETA = 10000.0
D_FF = 2816
CONV_WIDTH = 3
NORM_EPS = 1e-6
SUBLN_EPS = 1e-5

A_W = A_HEADS * HEAD_DIM
B_W = B_HEADS * HEAD_DIM
C_W = C_HEADS * HEAD_DIM
IN_COLS = 3 * A_W + 3 * B_W + 3 * C_W + C_HEADS
LANES = 128
IN_COLS_PAD = 3 * A_W + 3 * B_W + 3 * C_W + LANES

LOG2E = 1.4426950408889634
NEG = -1e30
VMEM_LIMIT = 56 * 2**20
BIAS_PIECES = 3
BIAS_LANES = 16

TM = 512
TQ = 512
DIL_TILE = 2048
BAND = 128
NCLS = DIL_TILE // BAND
DIL_UNROLL = 8


def _cparams(*sem):
    return pltpu.CompilerParams(dimension_semantics=sem, vmem_limit_bytes=VMEM_LIMIT)


def _nt_dot(a, b):
    return lax.dot_general(a, b, (((1,), (1,)), ((), ())), preferred_element_type=F32)


def _mod_kernel(c_ref, w_ref, b_ref, o_ref):
    c = c_ref[...]
    sc = c * (1.0 / (1.0 + jnp.exp(-c)))
    o_ref[0] = jnp.dot(sc, w_ref[0], preferred_element_type=F32,
                       precision=lax.Precision.HIGHEST) + b_ref[0]


def _modulation(c, w_mod, b_mod):
    depth, d, n = w_mod.shape
    bn = c.shape[0]
    nb = n // d
    return pl.pallas_call(
        _mod_kernel,
        grid=(depth, nb),
        in_specs=[
            pl.BlockSpec((bn, d), lambda l, j: (0, 0)),
            pl.BlockSpec((1, d, d), lambda l, j: (l, 0, j)),
            pl.BlockSpec((1, 1, d), lambda l, j: (l, 0, j)),
        ],
        out_specs=pl.BlockSpec((1, bn, d), lambda l, j: (l, 0, j)),
        out_shape=jax.ShapeDtypeStruct((depth, bn, n), F32),
        compiler_params=_cparams("parallel", "parallel"),
        name="modulation",
    )(c, w_mod, b_mod.reshape(depth, 1, n))


def _modulated_norm(x, g, scale, shift):
    ms = jnp.mean(x * x, axis=-1, keepdims=True)
    y = x * lax.rsqrt(ms + NORM_EPS) * g
    return y * (1.0 + scale) + shift


def _rope128(r, cos, sin_signed, group):
    half = group // 2
    lane = lax.broadcasted_iota(jnp.int32, r.shape, 1)
    first = (lane % group) < half
    rot = jnp.where(first, pltpu.roll(r, LANES - half, 1), pltpu.roll(r, half, 1))
    return r * cos + rot * sin_signed


def _inproj_kernel(x_ref, g_ref, sc_ref, sh_ref, w_ref, cosa_ref, sina_ref, cosb_ref, sinb_ref,
                   qa_ref, ka_ref, va_ref, qb_ref, kb_ref, vb_ref, qc_ref, kc_ref, vc_ref, fz_ref):
    h = _modulated_norm(x_ref[0], g_ref[...], sc_ref[0], sh_ref[0]).astype(BF16)
    cosa, sina = cosa_ref[...], sina_ref[...]
    cosb, sinb = cosb_ref[...], sinb_ref[...]
    qa_scale = DIFF_QK_DIM ** -0.5 * LOG2E
    qb_scale = HEAD_DIM ** -0.5 * LOG2E
    qc_scale = HEAD_DIM ** -0.5 * LOG2E

    ra = jnp.dot(h, w_ref[0, :, 0:3 * A_W], preferred_element_type=F32)
    for blk in range(A_W // LANES):
        sl = slice(blk * LANES, (blk + 1) * LANES)
        q = ra[:, blk * LANES:(blk + 1) * LANES]
        k = ra[:, A_W + blk * LANES:A_W + (blk + 1) * LANES]
        qa_ref[0, :, sl] = (_rope128(q, cosa, sina, DIFF_QK_DIM) * qa_scale).astype(BF16)
        ka_ref[0, :, sl] = _rope128(k, cosa, sina, DIFF_QK_DIM).astype(BF16)
    va_ref[0] = ra[:, 2 * A_W:3 * A_W].astype(BF16)

    off = 3 * A_W
    rb = jnp.dot(h, w_ref[0, :, off:off + 3 * B_W], preferred_element_type=F32)
    for blk in range(B_W // LANES):
        sl = slice(blk * LANES, (blk + 1) * LANES)
        q = rb[:, blk * LANES:(blk + 1) * LANES]
        k = rb[:, B_W + blk * LANES:B_W + (blk + 1) * LANES]
        qb_ref[0, :, sl] = _rope128(q, cosb, sinb, HEAD_DIM) * qb_scale
        kb_ref[0, :, sl] = _rope128(k, cosb, sinb, HEAD_DIM)
    vb_ref[0] = rb[:, 2 * B_W:3 * B_W]

    off = 3 * A_W + 3 * B_W
    rc = jnp.dot(h, w_ref[0, :, off:off + 3 * C_W + LANES], preferred_element_type=F32)
    qc_ref[0] = (rc[:, 0:C_W] * qc_scale).astype(BF16)
    kc_ref[0] = rc[:, C_W:2 * C_W].astype(BF16)
    vc_ref[0] = rc[:, 2 * C_W:3 * C_W].astype(BF16)
    fz_ref[0] = rc[:, 3 * C_W:3 * C_W + LANES]


def _layer_block(stacked, layer, **kwargs):
    return pl.BlockSpec((1,) + stacked.shape[1:], lambda b, i: (layer, 0, 0), **kwargs)


def _in_projection(x, g, scale, shift, w_in_p, layer, tabs):
    bn, t, d = x.shape
    cosa, sina, cosb, sinb = tabs
    tok = lambda w: pl.BlockSpec((1, TM, w), lambda b, i: (b, i, 0))
    per_b = pl.BlockSpec((1, 1, d), lambda b, i: (b, 0, 0))
    tab = pl.BlockSpec((TM, LANES), lambda b, i: (i, 0))
    widths = [A_W] * 3 + [B_W] * 3 + [C_W] * 3 + [LANES]
    dtypes = [BF16] * 3 + [F32] * 3 + [BF16] * 3 + [F32]
    return pl.pallas_call(
        _inproj_kernel,
        grid=(bn, t // TM),
        in_specs=[tok(d), pl.BlockSpec((1, d), lambda b, i: (0, 0)), per_b, per_b,
                  _layer_block(w_in_p, layer), tab, tab, tab, tab],
        out_specs=[tok(w) for w in widths],
        out_shape=[jax.ShapeDtypeStruct((bn, t, w), dt) for w, dt in zip(widths, dtypes)],
        compiler_params=_cparams("parallel", "parallel"),
        name="in_projection",
    )(x, g.reshape(1, d), scale, shift, w_in_p, cosa, sina, cosb, sinb)


def _cumlogf_kernel(z_ref, b_ref, u_ref, lb_ref, o_ref):
    z = z_ref[0] + b_ref[...]
    logf = jnp.minimum(z, 0.0) - jnp.log1p(jnp.exp(-jnp.abs(z)))
    hi = lax.Precision.HIGHEST
    within = jnp.dot(logf, u_ref[...], preferred_element_type=F32, precision=hi)
    before = jnp.dot(lb_ref[...], within, preferred_element_type=F32, precision=hi)
    cum = (within + before[:, LANES - 1:LANES]) * LOG2E
    rest = cum
    for piece in range(BIAS_PIECES):
        part = rest.astype(BF16).astype(F32)
        o_ref[0, piece] = part
        rest = rest - part


def _cum_logf(fz, forget_bias):
    bn, t, _ = fz.shape
    nc = t // LANES
    rows = 8 * nc
    z = jnp.transpose(fz[:, :, :8], (0, 2, 1)).reshape(bn, rows, LANES)
    bias = jnp.concatenate([forget_bias.astype(F32), jnp.zeros((8 - C_HEADS,), F32)])
    bias = jnp.broadcast_to(jnp.repeat(bias, nc)[:, None], (rows, LANES))
    idx = jnp.arange(LANES)
    upper = (idx[:, None] <= idx[None, :]).astype(F32)
    r = jnp.arange(rows)
    lblk = ((r[:, None] // nc == r[None, :] // nc) & (r[None, :] < r[:, None])).astype(F32)
    cum = pl.pallas_call(
        _cumlogf_kernel,
        grid=(bn,),
        in_specs=[pl.BlockSpec((1, rows, LANES), lambda b: (b, 0, 0)),
                  pl.BlockSpec((rows, LANES), lambda b: (0, 0)),
                  pl.BlockSpec((LANES, LANES), lambda b: (0, 0)),
                  pl.BlockSpec((rows, rows), lambda b: (0, 0))],
        out_specs=pl.BlockSpec((1, BIAS_PIECES, rows, LANES), lambda b: (b, 0, 0, 0)),
        out_shape=jax.ShapeDtypeStruct((bn, BIAS_PIECES, rows, LANES), F32),
        compiler_params=_cparams("parallel"),
        name="cum_logf",
    )(z, bias, upper, lblk)
    return cum.reshape(bn, BIAS_PIECES, 8, t)


def _bias_operands(cum):
    bn, _, _, t = cum.shape
    pieces = jnp.transpose(cum[:, :, :C_HEADS], (0, 2, 1, 3)).astype(BF16)
    ones = jnp.ones_like(pieces)
    pad = jnp.zeros((bn, C_HEADS, 8 - 2 * BIAS_PIECES, t), BF16)

    def pack(first, second):
        blk = jnp.concatenate([first, second, pad], axis=2).reshape(bn, C_HEADS // 2, BIAS_LANES, t)
        return jnp.transpose(blk, (0, 1, 3, 2))

    return pack(ones, pieces), pack(-pieces, ones)


def _to_lane_block(x):
    return jnp.pad(x, ((0, 0), (0, LANES - BIAS_LANES)))


def _lane_blocks(s):
    return [s[:, cb * LANES:(cb + 1) * LANES] for cb in range(s.shape[1] // LANES)]


def _lane_range(lane, lo, hi):
    return (lane >= lo) & (lane < hi)


def _own_half(lane, h):
    return lane < HEAD_DIM if h == 0 else lane >= HEAD_DIM


def _with_ones(v2, h):
    lane = lax.broadcasted_iota(jnp.int32, v2.shape, 1)
    return jnp.where(_own_half(lane, h), v2, jnp.ones_like(v2))


def _online_softmax(s, keep, m_prev):
    blocks = _lane_blocks(s)
    if keep is not None:
        blocks = [sb if kb is None else jnp.where(kb, sb, NEG) for kb, sb in zip(keep, blocks)]
    mpart = functools.reduce(jnp.maximum, blocks)
    m_new = jnp.maximum(m_prev, jnp.max(mpart, axis=-1, keepdims=True))
    alpha = jnp.exp2(m_prev - m_new)
    p = jnp.concatenate([jnp.exp2(sb - m_new).astype(BF16) for sb in blocks], axis=1)
    return m_new, alpha, p


class _Part(NamedTuple):
    r0: int
    nrows: int
    nkeys: int
    keep: Any


def _diagonal_parts(tq):
    half = tq // 2
    row = lax.broadcasted_iota(jnp.int32, (half, LANES), 0)
    lane = lax.broadcasted_iota(jnp.int32, (half, LANES), 1)
    tri = [lane + cb * LANES <= row for cb in range(half // LANES)]
    return [_Part(0, half, half, tri), _Part(half, half, tq, [None] * (half // LANES) + tri)]


def _causal_sweep(i, scores, consume, tq, group, scores_first):
    full = _Part(0, tq, tq, None)

    def run(work):
        if scores_first:
            tiles = [scores(j, part) for j, part in work]
            for s, (j, part) in zip(tiles, work):
                consume(s, j, part)
        else:
            for j, part in work:
                consume(scores(j, part), j, part)

    def body(p, carry):
        run([(group * p + k, full) for k in range(group)])
        return carry

    shift = group.bit_length() - 1
    assert group == 1 << shift and group >= 2
    lax.fori_loop(0, lax.shift_right_logical(i, shift), body, 0)
    half = group // 2
    while half >= 2:
        done = i & ~(2 * half - 1)

        @pl.when((i & half) != 0)
        def _(done=done, half=half):
            run([(done + k, full) for k in range(half)])

        half //= 2

    diagonal = [(i, part) for part in _diagonal_parts(tq)]
    odd = (i & 1) != 0

    @pl.when(odd)
    def _():
        run([(i - 1, full)] + diagonal)

    @pl.when(jnp.logical_not(odd))
    def _():
        run(diagonal)


def _stack_rows(qcat, nstack, tq, part):
    if part.nrows == tq:
        return qcat
    return jnp.concatenate([qcat[g * tq + part.r0:g * tq + part.r0 + part.nrows] for g in range(nstack)], axis=0)


def _normalize(acc):
    return acc / pltpu.roll(acc, HEAD_DIM, 1)


def _fox_kernel(q_ref, qx_ref, k_ref, kx_ref, v_ref, o_ref, m_sc, acc_sc, *, tq):
    i = pl.program_id(2)
    lane = lax.broadcasted_iota(jnp.int32, (tq, LANES), 1)
    q2, qx = q_ref[0], _to_lane_block(qx_ref[0, 0])
    zero = jnp.zeros_like(q2)
    qcat = jnp.concatenate(
        [jnp.concatenate([jnp.where(_own_half(lane, h), q2, zero),
                          jnp.where(_lane_range(lane, 8 * h, 8 * h + 8), qx, zero)], axis=1)
         for h in range(2)], axis=0)
    m_sc[...] = jnp.full(m_sc.shape, NEG, F32)
    acc_sc[...] = jnp.zeros(acc_sc.shape, F32)

    def key_rows(j, n):
        return pl.ds(pl.multiple_of(j * tq, tq), n)

    def scores(j, part):
        keys = key_rows(j, part.nkeys)
        kcat = jnp.concatenate([k_ref[0, keys, :], _to_lane_block(kx_ref[0, 0, keys, :])], axis=1)
        return _nt_dot(_stack_rows(qcat, 2, tq, part), kcat)

    def consume(s, j, part):
        rows = slice(part.r0, part.r0 + part.nrows)
        v2 = v_ref[0, key_rows(j, part.nkeys), :]
        for h in range(2):
            m_new, alpha, p = _online_softmax(s[h * part.nrows:(h + 1) * part.nrows], part.keep, m_sc[h, rows])
            pv = jnp.dot(p, _with_ones(v2, h), preferred_element_type=F32)
            acc_sc[h, rows] = alpha * acc_sc[h, rows] + pv
            m_sc[h, rows] = m_new

    _causal_sweep(i, scores, consume, tq, group=4, scores_first=True)

    o_ref[0] = jnp.where(lane < HEAD_DIM, _normalize(acc_sc[0]), _normalize(acc_sc[1])).astype(o_ref.dtype)


def _fox_attention(qc, kc, vc, q_aug, k_aug):
    bn, t, _ = qc.shape
    tq = TQ
    qblk = pl.BlockSpec((1, tq, LANES), lambda b, p, i: (b, i, p))
    full = pl.BlockSpec((1, t, LANES), lambda b, p, i: (b, 0, p))
    return pl.pallas_call(
        functools.partial(_fox_kernel, tq=tq),
        grid=(bn, C_HEADS // 2, t // tq),
        in_specs=[qblk, pl.BlockSpec((1, 1, tq, BIAS_LANES), lambda b, p, i: (b, p, i, 0)),
                  full, pl.BlockSpec((1, 1, t, BIAS_LANES), lambda b, p, i: (b, p, 0, 0)), full],
        out_specs=qblk,
        out_shape=jax.ShapeDtypeStruct((bn, t, C_W), BF16),
        scratch_shapes=[pltpu.VMEM((2, tq, LANES), F32)] * 2,
        compiler_params=_cparams("parallel", "parallel", "arbitrary"),
        name="forgetting_attention",
    )(qc, q_aug, kc, k_aug, vc)


def _diff_kernel(q_ref, k_ref, v_ref, lp_ref, g_ref, o_ref, m_sc, acc_sc, *, tq, lam_init):
    i = pl.program_id(2)
    lane = lax.broadcasted_iota(jnp.int32, (tq, LANES), 1)
    q2 = q_ref[0]
    zero = jnp.zeros_like(q2)
    nmap = LANES // DIFF_QK_DIM
    qcat = jnp.concatenate(
        [jnp.where(_lane_range(lane, g * DIFF_QK_DIM, (g + 1) * DIFF_QK_DIM), q2, zero)
         for g in range(nmap)], axis=0)
    m_sc[...] = jnp.full(m_sc.shape, NEG, F32)
    acc_sc[...] = jnp.zeros(acc_sc.shape, F32)

    def key_rows(j, n):
        return pl.ds(pl.multiple_of(j * tq, tq), n)

    def scores(j, part):
        return _nt_dot(_stack_rows(qcat, nmap, tq, part), k_ref[0, key_rows(j, part.nkeys), :])

    def consume(s, j, part):
        rows = slice(part.r0, part.r0 + part.nrows)
        nr = part.nrows
        v2 = v_ref[0, key_rows(j, part.nkeys), :]
        for h in range(2):
            stats = [_online_softmax(s[g * nr:(g + 1) * nr], part.keep, m_sc[g, rows]) for g in (2 * h, 2 * h + 1)]
            p = jnp.concatenate([st[2] for st in stats], axis=0)
            pv = jnp.dot(p, _with_ones(v2, h), preferred_element_type=F32)
            for n, g in enumerate((2 * h, 2 * h + 1)):
                acc_sc[g, rows] = stats[n][1] * acc_sc[g, rows] + pv[n * nr:(n + 1) * nr]
                m_sc[g, rows] = stats[n][0]

    _causal_sweep(i, scores, consume, tq, group=4, scores_first=False)

    lp = lp_ref[...]
    lam = (jnp.exp(jnp.sum(lp[0:1] * lp[1:2], axis=-1, keepdims=True))
           - jnp.exp(jnp.sum(lp[2:3] * lp[3:4], axis=-1, keepdims=True)) + lam_init)
    norm = [_normalize(acc_sc[g]) for g in range(nmap)]
    first = lane < HEAD_DIM
    o = jnp.where(first, norm[0] - lam * norm[1], norm[2] - lam * norm[3])
    osq = o * o
    ms0 = jnp.sum(jnp.where(first, osq, 0.0), axis=-1, keepdims=True) * (1.0 / HEAD_DIM)
    ms1 = jnp.sum(jnp.where(first, 0.0, osq), axis=-1, keepdims=True) * (1.0 / HEAD_DIM)
    o = o * lax.rsqrt(jnp.where(first, ms0, ms1) + SUBLN_EPS)
    o_ref[0] = (o * g_ref[...] * (1.0 - lam_init)).astype(o_ref.dtype)


def _diff_attention(qa, ka, va, lam_params, subln_g, lam_init):
    bn, t, _ = qa.shape
    tq = TQ
    g2 = jnp.concatenate([subln_g, subln_g]).reshape(1, LANES).astype(F32)
    qblk = pl.BlockSpec((1, tq, LANES), lambda b, p, i: (b, i, p))
    full = pl.BlockSpec((1, t, LANES), lambda b, p, i: (b, 0, p))
    return pl.pallas_call(
        functools.partial(_diff_kernel, tq=tq, lam_init=lam_init),
        grid=(bn, A_HEADS // 2, t // tq),
        in_specs=[qblk, full, full,
                  pl.BlockSpec((4, DIFF_QK_DIM), lambda b, p, i: (0, 0)),
                  pl.BlockSpec((1, LANES), lambda b, p, i: (0, 0))],
        out_specs=qblk,
        out_shape=jax.ShapeDtypeStruct((bn, t, A_W), BF16),
        scratch_shapes=[pltpu.VMEM((4, tq, LANES), F32)] * 2,
        compiler_params=_cparams("parallel", "parallel", "arbitrary"),
        name="diff_attention",
    )(qa, ka, va, lam_params.astype(F32), g2)


SPLIT = 4


def _class_block(cls):
    return SPLIT * (cls % SPLIT) + cls // SPLIT


def _class_runs(u, r, d):
    g = NCLS // d
    run = BAND // g
    return [(_class_block(r + d * b) * BAND + run * u, run) for b in range(g)]


def _dilated_kernel(q_ref, k_ref, v_ref, o_ref, tmp, qcm, kcm, vcm, m_sc, acc_sc, *, tile):
    i = pl.program_id(2)
    cur = i & 1
    prev = 1 - cur
    assert tile == NCLS * BAND and NCLS == SPLIT * SPLIT

    @pl.when(i == 0)
    def _():
        kcm[1] = jnp.zeros((tile, LANES), F32)
        vcm[1] = jnp.zeros((tile, LANES), F32)

    part = tile // SPLIT

    def regroup(src_ref, store):
        for c1 in range(SPLIT):
            tmp[c1 * part:(c1 + 1) * part, :] = src_ref[0, pl.ds(c1, part, stride=SPLIT), :]
        for blk in range(NCLS):
            c1, c2 = divmod(blk, SPLIT)
            store(slice(blk * BAND, (blk + 1) * BAND), tmp[pl.ds(c1 * part + c2, BAND, stride=SPLIT), :])

    def store_q(dst, val):
        qcm[dst, :] = val

    def store_k(dst, val):
        kcm[cur, dst, :] = val

    def store_v(dst, val):
        vcm[cur, dst, :] = val

    regroup(q_ref, store_q)
    regroup(k_ref, store_k)
    regroup(v_ref, store_v)

    row = lax.broadcasted_iota(jnp.int32, (BAND, LANES), 0)
    lane = lax.broadcasted_iota(jnp.int32, (BAND, LANES), 1)

    def run_rows(start, n):
        return pl.ds(pl.multiple_of(start, 8), n)

    def gather(ref, slot, runs):
        pieces = [ref[run_rows(s, n), :] if slot is None else ref[slot, run_rows(s, n), :] for s, n in runs]
        return pieces[0] if len(pieces) == 1 else jnp.concatenate(pieces, axis=0)

    bodies = []
    for pat, (window, d) in enumerate(DILATED_PATTERNS):
        assert window == BAND * d and NCLS % d == 0
        g = NCLS // d
        run = BAND // g
        nq = g * (row % run) + row // run
        nk = g * (lane % run) + lane // run
        bias_cur = jnp.where(nk <= nq, 0.0, NEG)
        bias_prev = jnp.where(nk >= nq, 0.0, NEG)
        bias_first = jnp.where(i > 0, bias_prev, NEG)

        def body(idx, d=d, g=g, pat=pat, bias_cur=bias_cur, bias_prev=bias_prev, bias_first=bias_first):
            u = idx // d
            r = idx - u * d
            first = u == 0
            runs = _class_runs(u, r, d)
            pruns = _class_runs(jnp.where(first, g - 1, u - 1), r, d)
            pslot = jnp.where(first, prev, cur)
            q = gather(qcm, None, runs).astype(BF16)
            zero = jnp.zeros_like(q)
            qcat = jnp.concatenate([jnp.where(_own_half(lane, h), q, zero) for h in range(2)], axis=0)
            kcat = jnp.concatenate([gather(kcm, pslot, pruns), gather(kcm, cur, runs)], axis=0).astype(BF16)
            vcat = jnp.concatenate([gather(vcm, pslot, pruns), gather(vcm, cur, runs)], axis=0).astype(BF16)
            bias = jnp.concatenate([jnp.where(first, bias_first, bias_prev), bias_cur], axis=1)
            s = _nt_dot(qcat, kcat)
            for h in range(2):
                sh = s[h * BAND:(h + 1) * BAND] + bias
                m = jnp.broadcast_to(jnp.max(sh, axis=-1, keepdims=True), (BAND, LANES))
                p = jnp.concatenate([jnp.exp2(sb - m).astype(BF16) for sb in _lane_blocks(sh)], axis=1)
                pv = jnp.dot(p, _with_ones(vcat, h), preferred_element_type=F32)
                off = 0
                for start, n in runs:
                    acc_sc[2 * pat + h, run_rows(start, n), :] = pv[off:off + n]
                    m_sc[2 * pat + h, run_rows(start, n), :] = m[off:off + n]
                    off += n

        bodies.append(body)

    def group(it, carry):
        for k in range(DIL_UNROLL):
            for body in bodies:
                body(it * DIL_UNROLL + k)
        return carry

    lax.fori_loop(0, NCLS // DIL_UNROLL, group, 0)

    lane_t = lax.broadcasted_iota(jnp.int32, (tile, LANES), 1)
    npat = len(DILATED_PATTERNS)
    outs = []
    for h in range(2):
        ms = [m_sc[2 * pat + h] for pat in range(npat)]
        m_all = functools.reduce(jnp.maximum, ms)
        tot = sum(jnp.exp2(ms[pat] - m_all) * acc_sc[2 * pat + h] for pat in range(npat))
        outs.append(_normalize(tot))
    res = jnp.where(lane_t < HEAD_DIM, outs[0], outs[1])
    for c in range(NCLS):
        blk = _class_block(c)
        o_ref[0, pl.ds(c, BAND, stride=NCLS), :] = res[blk * BAND:(blk + 1) * BAND]


def _dilated_attention(qb, kb, vb):
    bn, t, _ = qb.shape
    tile = DIL_TILE
    npat = len(DILATED_PATTERNS)
    blk = pl.BlockSpec((1, tile, LANES), lambda b, p, i: (b, i, p))
    return pl.pallas_call(
        functools.partial(_dilated_kernel, tile=tile),
        grid=(bn, B_HEADS // 2, t // tile),
        in_specs=[blk, blk, blk],
        out_specs=blk,
        out_shape=jax.ShapeDtypeStruct((bn, t, B_W), F32),
        scratch_shapes=[pltpu.VMEM((tile, LANES), F32)] * 2 + [pltpu.VMEM((2, tile, LANES), F32),
                        pltpu.VMEM((2, tile, LANES), F32)] + [pltpu.VMEM((2 * npat, tile, LANES), F32)] * 2,
        compiler_params=_cparams("parallel", "parallel", "arbitrary"),
        name="dilated_attention",
    )(qb, kb, vb)


FF_CHUNK = 256
HALO = 8


def _resident(shape):
    return pl.BlockSpec(shape, lambda b, i: (0,) * len(shape), pipeline_mode=pl.Buffered(1))


def _mix_up_kernel(oa_ref, ob_ref, oc_ref, x_ref, wo_ref, gate_ref, g_ref, sc_ref, sh_ref,
                   w_ref, cw_ref, cb_ref, x1_ref, y_ref, halo_sc, g_sc):
    i = pl.program_id(1)

    @pl.when(i == 0)
    def _():
        halo_sc[...] = jnp.zeros(halo_sc.shape, F32)

    o = jnp.concatenate([oa_ref[0], ob_ref[0].astype(BF16), oc_ref[0]], axis=-1)
    x1 = x_ref[0] + gate_ref[0] * jnp.dot(o, wo_ref[0], preferred_element_type=F32)
    x1_ref[0] = x1
    h = _modulated_norm(x1, g_ref[...], sc_ref[0], sh_ref[0]).astype(BF16)
    tm = h.shape[0]
    for c in range(D_FF // FF_CHUNK):
        cs = slice(c * FF_CHUNK, (c + 1) * FF_CHUNK)
        u = jnp.dot(h, w_ref[0, :, cs], preferred_element_type=F32)
        g = jnp.dot(h, w_ref[0, :, D_FF + c * FF_CHUNK:D_FF + (c + 1) * FF_CHUNK], preferred_element_type=F32)
        g_sc[0:HALO, :] = halo_sc[:, cs]
        g_sc[HALO:HALO + tm, :] = g
        halo_sc[:, cs] = g[tm - HALO:tm, :]
        conv = (cb_ref[:, cs] + g_sc[HALO - 2:HALO - 2 + tm, :] * cw_ref[0:1, cs]
                + g_sc[HALO - 1:HALO - 1 + tm, :] * cw_ref[1:2, cs] + g * cw_ref[2:3, cs])
        act = conv * (1.0 / (1.0 + jnp.exp(-conv)))
        y_ref[0, :, cs] = (act * u).astype(BF16)


def _mix_up_projection(oa, ob, oc, x, w_out, gate1, g_mlp, scale2, shift2, w_up, layer, conv_w, conv_b):
    bn, t, d = x.shape
    tok = lambda w: pl.BlockSpec((1, TM, w), lambda b, i: (b, i, 0))
    per_b = pl.BlockSpec((1, 1, d), lambda b, i: (b, 0, 0))
    once = dict(pipeline_mode=pl.Buffered(1))
    return pl.pallas_call(
        _mix_up_kernel,
        grid=(bn, t // TM),
        in_specs=[tok(A_W), tok(B_W), tok(C_W), tok(d), _layer_block(w_out, layer, **once), per_b,
                  _resident((1, d)), per_b, per_b,
                  _layer_block(w_up, layer, **once), _resident((CONV_WIDTH, D_FF)), _resident((1, D_FF))],
        out_specs=[tok(d), tok(D_FF)],
        out_shape=[jax.ShapeDtypeStruct((bn, t, d), F32), jax.ShapeDtypeStruct((bn, t, D_FF), BF16)],
        scratch_shapes=[pltpu.VMEM((HALO, D_FF), F32), pltpu.VMEM((HALO + TM, FF_CHUNK), F32)],
        compiler_params=_cparams("parallel", "arbitrary"),
        name="mix_up_projection",
    )(oa, ob, oc, x, w_out, gate1, g_mlp.reshape(1, d), scale2, shift2, w_up, conv_w, conv_b.reshape(1, D_FF))


def _down_kernel(y_ref, x_ref, w_ref, gate_ref, g_ref, o_ref, *, final_norm):
    x2 = x_ref[0] + gate_ref[0] * jnp.dot(y_ref[0], w_ref[0], preferred_element_type=F32)
    if final_norm:
        ms = jnp.mean(x2 * x2, axis=-1, keepdims=True)
        x2 = x2 * lax.rsqrt(ms + NORM_EPS) * g_ref[...]
    o_ref[0] = x2


def _down_projection(y, x1, w_down, layer, gate2, g_final, final_norm):
    bn, t, d = x1.shape
    return pl.pallas_call(
        functools.partial(_down_kernel, final_norm=final_norm),
        grid=(bn, t // TM),
        in_specs=[pl.BlockSpec((1, TM, D_FF), lambda b, i: (b, i, 0)),
                  pl.BlockSpec((1, TM, d), lambda b, i: (b, i, 0)),
                  _layer_block(w_down, layer),
                  pl.BlockSpec((1, 1, d), lambda b, i: (b, 0, 0)),
                  pl.BlockSpec((1, d), lambda b, i: (0, 0))],
        out_specs=pl.BlockSpec((1, TM, d), lambda b, i: (b, i, 0)),
        out_shape=jax.ShapeDtypeStruct((bn, t, d), F32),
        compiler_params=_cparams("parallel", "parallel"),
        name="down_projection",
    )(y, x1, w_down, gate2, g_final.reshape(1, d))


def _rope_table(t, dim):
    inv = 1.0 / (ROPE_THETA ** (jnp.arange(0, dim, 2, dtype=F32) / dim))
    ang = jnp.arange(t, dtype=F32)[:, None] * inv[None, :]
    cos, sin = jnp.cos(ang), jnp.sin(ang)
    reps = LANES // dim
    return (jnp.tile(jnp.concatenate([cos, cos], axis=-1), (1, reps)),
            jnp.tile(jnp.concatenate([-sin, sin], axis=-1), (1, reps)))


def kernel(x, c, w_mod, b_mod, g_attn, w_in, diff_lambda, subln_g, forget_bias, w_out,
           g_mlp, w_up, conv_w, conv_b, w_down, g_final):
    depth = w_mod.shape[0]
    bn, t, d = x.shape
    assert d == D_MODEL and t % DIL_TILE == 0 and t % TM == 0 and t % TQ == 0
    tabs = _rope_table(t, DIFF_QK_DIM) + _rope_table(t, HEAD_DIM)
    mod = _modulation(c, w_mod, b_mod)
    w_in_p = jnp.pad(w_in, ((0, 0), (0, 0), (0, IN_COLS_PAD - IN_COLS))).astype(BF16)
    w_out_b, w_up_b, w_down_b = w_out.astype(BF16), w_up.astype(BF16), w_down.astype(BF16)

    for layer in range(depth):
        lam_init = 0.8 - 0.6 * math.exp(-0.3 * layer)
        shift1, scale1, gate1, shift2, scale2, gate2 = [
            m[:, None, :] for m in jnp.split(mod[layer], 6, axis=-1)]
        qa, ka, va, qb, kb, vb, qc, kc, vc, fz = _in_projection(
            x, g_attn[layer], scale1, shift1, w_in_p, layer, tabs)
        q_aug, k_aug = _bias_operands(_cum_logf(fz, forget_bias[layer]))
        oa = _diff_attention(qa, ka, va, diff_lambda[layer], subln_g[layer], lam_init)
        ob = _dilated_attention(qb, kb, vb)
        oc = _fox_attention(qc, kc, vc, q_aug, k_aug)
        x1, y = _mix_up_projection(oa, ob, oc, x, w_out_b, gate1, g_mlp[layer], scale2, shift2,
                                   w_up_b, layer, conv_w[layer], conv_b[layer])
        x = _down_projection(y, x1, w_down_b, layer, gate2, g_final, layer == depth - 1)
    return x
```

```python
import functools
import math
from typing import Any, NamedTuple

import jax
import jax.numpy as jnp
from jax import lax
from jax.experimental import pallas as pl
from jax.experimental.pallas import tpu as pltpu

F32 = jnp.float32
BF16 = jnp.bfloat16

D_MODEL = 1024
HEAD_DIM = 64
A_HEADS = 4
DIFF_QK_DIM = 32
B_HEADS = 6
C_HEADS = 6
DILATED_PATTERNS = ((128, 1), (512, 4), (2048, 16))
ROPE_THETA = 10000.0
D_FF = 2816
CONV_WIDTH = 3
NORM_EPS = 1e-6
SUBLN_EPS = 1e-5

A_W = A_HEADS * HEAD_DIM
B_W = B_HEADS * HEAD_DIM
C_W = C_HEADS * HEAD_DIM
IN_COLS = 3 * A_W + 3 * B_W + 3 * C_W + C_HEADS
LANES = 128
IN_COLS_PAD = 3 * A_W + 3 * B_W + 3 * C_W + LANES

LOG2E = 1.4426950408889634
NEG = -1e30
VMEM_LIMIT = 56 * 2**20
BIAS_PIECES = 3
FZ_ROWS = 8
BIAS_LANES = 16

TM = 512
TQ = 512
DIL_TILE = 2048
BAND = 128
NCLS = DIL_TILE // BAND
DIL_UNROLL = 8


def _cparams(*sem):
    return pltpu.CompilerParams(dimension_semantics=sem, vmem_limit_bytes=VMEM_LIMIT)


def _nt_dot(a, b):
    return lax.dot_general(a, b, (((1,), (1,)), ((), ())), preferred_element_type=F32)


def _mod_kernel(c_ref, w_ref, b_ref, o_ref):
    c = c_ref[...]
    sc = c * (1.0 / (1.0 + jnp.exp(-c)))
    o_ref[0] = jnp.dot(sc, w_ref[0], preferred_element_type=F32,
                       precision=lax.Precision.HIGHEST) + b_ref[0]


def _modulation(c, w_mod, b_mod):
    depth, d, n = w_mod.shape
    bn = c.shape[0]
    nb = n // d
    return pl.pallas_call(
        _mod_kernel,
        grid=(depth, nb),
        in_specs=[
            pl.BlockSpec((bn, d), lambda l, j: (0, 0)),
            pl.BlockSpec((1, d, d), lambda l, j: (l, 0, j)),
            pl.BlockSpec((1, 1, d), lambda l, j: (l, 0, j)),
        ],
        out_specs=pl.BlockSpec((1, bn, d), lambda l, j: (l, 0, j)),
        out_shape=jax.ShapeDtypeStruct((depth, bn, n), F32),
        compiler_params=_cparams("parallel", "parallel"),
        name="modulation",
    )(c, w_mod, b_mod.reshape(depth, 1, n))


def _modulated_norm(x, g, scale, shift):
    ms = jnp.mean(x * x, axis=-1, keepdims=True)
    y = x * lax.rsqrt(ms + NORM_EPS) * g
    return y * (1.0 + scale) + shift


def _rope128(r, cos, sin_signed, group):
    half = group // 2
    lane = lax.broadcasted_iota(jnp.int32, r.shape, 1)
    first = (lane % group) < half
    rot = jnp.where(first, pltpu.roll(r, LANES - half, 1), pltpu.roll(r, half, 1))
    return r * cos + rot * sin_signed


def _inproj_kernel(x_ref, g_ref, sc_ref, sh_ref, w_ref, cosa_ref, sina_ref, cosb_ref, sinb_ref,
                   qa_ref, ka_ref, va_ref, qb_ref, kb_ref, vb_ref, qc_ref, kc_ref, vc_ref, fz_ref):
    h = _modulated_norm(x_ref[0], g_ref[...], sc_ref[0], sh_ref[0]).astype(BF16)
    cosa, sina = cosa_ref[...], sina_ref[...]
    cosb, sinb = cosb_ref[...], sinb_ref[...]
    qa_scale = DIFF_QK_DIM ** -0.5 * LOG2E
    qb_scale = HEAD_DIM ** -0.5 * LOG2E
    qc_scale = HEAD_DIM ** -0.5 * LOG2E

    ra = jnp.dot(h, w_ref[0, :, 0:3 * A_W], preferred_element_type=F32)
    for blk in range(A_W // LANES):
        sl = slice(blk * LANES, (blk + 1) * LANES)
        q = ra[:, blk * LANES:(blk + 1) * LANES]
        k = ra[:, A_W + blk * LANES:A_W + (blk + 1) * LANES]
        qa_ref[0, :, sl] = (_rope128(q, cosa, sina, DIFF_QK_DIM) * qa_scale).astype(BF16)
        ka_ref[0, :, sl] = _rope128(k, cosa, sina, DIFF_QK_DIM).astype(BF16)
    va_ref[0] = ra[:, 2 * A_W:3 * A_W].astype(BF16)

    off = 3 * A_W
    rb = jnp.dot(h, w_ref[0, :, off:off + 3 * B_W], preferred_element_type=F32)
    for blk in range(B_W // LANES):
        sl = slice(blk * LANES, (blk + 1) * LANES)
        q = rb[:, blk * LANES:(blk + 1) * LANES]
        k = rb[:, B_W + blk * LANES:B_W + (blk + 1) * LANES]
        qb_ref[0, :, sl] = _rope128(q, cosb, sinb, HEAD_DIM) * qb_scale
        kb_ref[0, :, sl] = _rope128(k, cosb, sinb, HEAD_DIM)
    vb_ref[0] = rb[:, 2 * B_W:3 * B_W]

    off = 3 * A_W + 3 * B_W
    rc = jnp.dot(h, w_ref[0, :, off:off + 3 * C_W + LANES], preferred_element_type=F32)
    qc_ref[0] = (rc[:, 0:C_W] * qc_scale).astype(BF16)
    kc_ref[0] = rc[:, C_W:2 * C_W].astype(BF16)
    vc_ref[0] = rc[:, 2 * C_W:3 * C_W].astype(BF16)
    fz_ref[0] = jnp.transpose(rc[:, 3 * C_W:3 * C_W + LANES])[0:FZ_ROWS]


def _layer_block(stacked, layer, **kwargs):
    return pl.BlockSpec((1,) + stacked.shape[1:], lambda b, i: (layer, 0, 0), **kwargs)


def _in_projection(x, g, scale, shift, w_in_p, layer, tabs):
    bn, t, d = x.shape
    cosa, sina, cosb, sinb = tabs
    tok = lambda w: pl.BlockSpec((1, TM, w), lambda b, i: (b, i, 0))
    per_b = pl.BlockSpec((1, 1, d), lambda b, i: (b, 0, 0))
    tab = pl.BlockSpec((TM, LANES), lambda b, i: (i, 0))
    widths = [A_W] * 3 + [B_W] * 3 + [C_W] * 3
    dtypes = [BF16] * 3 + [F32] * 3 + [BF16] * 3
    return pl.pallas_call(
        _inproj_kernel,
        grid=(bn, t // TM),
        in_specs=[tok(d), pl.BlockSpec((1, d), lambda b, i: (0, 0)), per_b, per_b,
                  _layer_block(w_in_p, layer), tab, tab, tab, tab],
        out_specs=[tok(w) for w in widths] + [pl.BlockSpec((1, FZ_ROWS, TM), lambda b, i: (b, 0, i))],
        out_shape=[jax.ShapeDtypeStruct((bn, t, w), dt) for w, dt in zip(widths, dtypes)]
        + [jax.ShapeDtypeStruct((bn, FZ_ROWS, t), F32)],
        compiler_params=_cparams("parallel", "parallel"),
        name="in_projection",
    )(x, g.reshape(1, d), scale, shift, w_in_p, cosa, sina, cosb, sinb)


def _cumlogf_kernel(z_ref, b_ref, u_ref, lb_ref, o_ref):
    z = z_ref[0] + b_ref[...]
    logf = jnp.minimum(z, 0.0) - jnp.log1p(jnp.exp(-jnp.abs(z)))
    hi = lax.Precision.HIGHEST
    within = jnp.dot(logf, u_ref[...], preferred_element_type=F32, precision=hi)
    before = jnp.dot(lb_ref[...], within, preferred_element_type=F32, precision=hi)
    cum = (within + before[:, LANES - 1:LANES]) * LOG2E
    rest = cum
    for piece in range(BIAS_PIECES):
        part = rest.astype(BF16).astype(F32)
        o_ref[0, piece] = part
        rest = rest - part


def _cum_logf(fz, forget_bias):
    bn, _, t = fz.shape
    nc = t // LANES
    rows = FZ_ROWS * nc
    z = fz.reshape(bn, rows, LANES)
    bias = jnp.concatenate([forget_bias.astype(F32), jnp.zeros((FZ_ROWS - C_HEADS,), F32)])
    bias = jnp.broadcast_to(jnp.repeat(bias, nc)[:, None], (rows, LANES))
    idx = jnp.arange(LANES)
    upper = (idx[:, None] <= idx[None, :]).astype(F32)
    r = jnp.arange(rows)
    lblk = ((r[:, None] // nc == r[None, :] // nc) & (r[None, :] < r[:, None])).astype(F32)
    cum = pl.pallas_call(
        _cumlogf_kernel,
        grid=(bn,),
        in_specs=[pl.BlockSpec((1, rows, LANES), lambda b: (b, 0, 0)),
                  pl.BlockSpec((rows, LANES), lambda b: (0, 0)),
                  pl.BlockSpec((LANES, LANES), lambda b: (0, 0)),
                  pl.BlockSpec((rows, rows), lambda b: (0, 0))],
        out_specs=pl.BlockSpec((1, BIAS_PIECES, rows, LANES), lambda b: (b, 0, 0, 0)),
        out_shape=jax.ShapeDtypeStruct((bn, BIAS_PIECES, rows, LANES), F32),
        compiler_params=_cparams("parallel"),
        name="cum_logf",
    )(z, bias, upper, lblk)
    return cum.reshape(bn, BIAS_PIECES, FZ_ROWS, t)


def _bias_operands(cum):
    bn, _, _, t = cum.shape
    pieces = jnp.transpose(cum[:, :, :C_HEADS], (0, 2, 1, 3)).astype(BF16)
    ones = jnp.ones_like(pieces)
    pad = jnp.zeros((bn, C_HEADS, BIAS_LANES // 2 - 2 * BIAS_PIECES, t), BF16)

    def pack(first, second):
        blk = jnp.concatenate([first, second, pad], axis=2).reshape(bn, C_HEADS // 2, BIAS_LANES, t)
        return jnp.transpose(blk, (0, 1, 3, 2))

    return pack(ones, pieces), pack(-pieces, ones)


def _to_lane_block(x):
    return jnp.pad(x, ((0, 0), (0, LANES - BIAS_LANES)))


def _lane_blocks(s):
    return [s[:, cb * LANES:(cb + 1) * LANES] for cb in range(s.shape[1] // LANES)]


def _lane_range(lane, lo, hi):
    return (lane >= lo) & (lane < hi)


def _own_half(lane, h):
    return lane < HEAD_DIM if h == 0 else lane >= HEAD_DIM


def _with_ones(v2, h):
    lane = lax.broadcasted_iota(jnp.int32, v2.shape, 1)
    return jnp.where(_own_half(lane, h), v2, jnp.ones_like(v2))


def _online_softmax(s, keep, m_prev):
    blocks = _lane_blocks(s)
    if keep is not None:
        blocks = [sb if kb is None else jnp.where(kb, sb, NEG) for kb, sb in zip(keep, blocks)]
    mpart = functools.reduce(jnp.maximum, blocks)
    m_new = jnp.maximum(m_prev, jnp.max(mpart, axis=-1, keepdims=True))
    alpha = jnp.exp2(m_prev - m_new)
    p = jnp.concatenate([jnp.exp2(sb - m_new).astype(BF16) for sb in blocks], axis=1)
    return m_new, alpha, p


class _Part(NamedTuple):
    r0: int
    nrows: int
    nkeys: int
    keep: Any


def _diagonal_parts(tq):
    half = tq // 2
    row = lax.broadcasted_iota(jnp.int32, (half, LANES), 0)
    lane = lax.broadcasted_iota(jnp.int32, (half, LANES), 1)
    tri = [lane + cb * LANES <= row for cb in range(half // LANES)]
    return [_Part(0, half, half, tri), _Part(half, half, tq, [None] * (half // LANES) + tri)]


def _causal_sweep(i, scores, consume, tq, group, scores_first):
    full = _Part(0, tq, tq, None)

    def run(work):
        if scores_first:
            tiles = [scores(j, part) for j, part in work]
            for s, (j, part) in zip(tiles, work):
                consume(s, j, part)
        else:
            for j, part in work:
                consume(scores(j, part), j, part)

    def body(p, carry):
        run([(group * p + k, full) for k in range(group)])
        return carry

    shift = group.bit_length() - 1
    assert group == 1 << shift and group >= 2
    lax.fori_loop(0, lax.shift_right_logical(i, shift), body, 0)
    half = group // 2
    while half >= 2:
        done = i & ~(2 * half - 1)

        @pl.when((i & half) != 0)
        def _(done=done, half=half):
            run([(done + k, full) for k in range(half)])

        half //= 2

    diagonal = [(i, part) for part in _diagonal_parts(tq)]
    odd = (i & 1) != 0

    @pl.when(odd)
    def _():
        run([(i - 1, full)] + diagonal)

    @pl.when(jnp.logical_not(odd))
    def _():
        run(diagonal)


def _stack_rows(qcat, nstack, tq, part):
    if part.nrows == tq:
        return qcat
    return jnp.concatenate([qcat[g * tq + part.r0:g * tq + part.r0 + part.nrows] for g in range(nstack)], axis=0)


def _normalize(acc):
    return acc / pltpu.roll(acc, HEAD_DIM, 1)


def _fox_kernel(q_ref, qx_ref, k_ref, kx_ref, v_ref, o_ref, m_sc, acc_sc, *, tq):
    i = pl.program_id(2)
    lane = lax.broadcasted_iota(jnp.int32, (tq, LANES), 1)
    q2, qx = q_ref[0], _to_lane_block(qx_ref[0, 0])
    zero = jnp.zeros_like(q2)
    qcat = jnp.concatenate(
        [jnp.concatenate([jnp.where(_own_half(lane, h), q2, zero),
                          jnp.where(_lane_range(lane, 8 * h, 8 * h + 8), qx, zero)], axis=1)
         for h in range(2)], axis=0)
    m_sc[...] = jnp.full(m_sc.shape, NEG, F32)
    acc_sc[...] = jnp.zeros(acc_sc.shape, F32)

    def key_rows(j, n):
        return pl.ds(pl.multiple_of(j * tq, tq), n)

    def scores(j, part):
        keys = key_rows(j, part.nkeys)
        kcat = jnp.concatenate([k_ref[0, keys, :], _to_lane_block(kx_ref[0, 0, keys, :])], axis=1)
        return _nt_dot(_stack_rows(qcat, 2, tq, part), kcat)

    def consume(s, j, part):
        rows = slice(part.r0, part.r0 + part.nrows)
        v2 = v_ref[0, key_rows(j, part.nkeys), :]
        for h in range(2):
            m_new, alpha, p = _online_softmax(s[h * part.nrows:(h + 1) * part.nrows], part.keep, m_sc[h, rows])
            pv = jnp.dot(p, _with_ones(v2, h), preferred_element_type=F32)
            acc_sc[h, rows] = alpha * acc_sc[h, rows] + pv
            m_sc[h, rows] = m_new

    _causal_sweep(i, scores, consume, tq, group=4, scores_first=True)

    o_ref[0] = jnp.where(lane < HEAD_DIM, _normalize(acc_sc[0]), _normalize(acc_sc[1])).astype(o_ref.dtype)


def _fox_attention(qc, kc, vc, q_aug, k_aug):
    bn, t, _ = qc.shape
    tq = TQ
    qblk = pl.BlockSpec((1, tq, LANES), lambda b, p, i: (b, i, p))
    full = pl.BlockSpec((1, t, LANES), lambda b, p, i: (b, 0, p))
    return pl.pallas_call(
        functools.partial(_fox_kernel, tq=tq),
        grid=(bn, C_HEADS // 2, t // tq),
        in_specs=[qblk, pl.BlockSpec((1, 1, tq, BIAS_LANES), lambda b, p, i: (b, p, i, 0)),
                  full, pl.BlockSpec((1, 1, t, BIAS_LANES), lambda b, p, i: (b, p, 0, 0)), full],
        out_specs=qblk,
        out_shape=jax.ShapeDtypeStruct((bn, t, C_W), BF16),
        scratch_shapes=[pltpu.VMEM((2, tq, LANES), F32)] * 2,
        compiler_params=_cparams("parallel", "parallel", "arbitrary"),
        name="forgetting_attention",
    )(qc, q_aug, kc, k_aug, vc)


def _diff_kernel(q_ref, k_ref, v_ref, lp_ref, g_ref, o_ref, m_sc, acc_sc, *, tq, lam_init):
    i = pl.program_id(2)
    lane = lax.broadcasted_iota(jnp.int32, (tq, LANES), 1)
    q2 = q_ref[0]
    zero = jnp.zeros_like(q2)
    nmap = LANES // DIFF_QK_DIM
    qcat = jnp.concatenate(
        [jnp.where(_lane_range(lane, g * DIFF_QK_DIM, (g + 1) * DIFF_QK_DIM), q2, zero)
         for g in range(nmap)], axis=0)
    m_sc[...] = jnp.full(m_sc.shape, NEG, F32)
    acc_sc[...] = jnp.zeros(acc_sc.shape, F32)

    def key_rows(j, n):
        return pl.ds(pl.multiple_of(j * tq, tq), n)

    def scores(j, part):
        return _nt_dot(_stack_rows(qcat, nmap, tq, part), k_ref[0, key_rows(j, part.nkeys), :])

    def consume(s, j, part):
        rows = slice(part.r0, part.r0 + part.nrows)
        nr = part.nrows
        v2 = v_ref[0, key_rows(j, part.nkeys), :]
        for h in range(2):
            stats = [_online_softmax(s[g * nr:(g + 1) * nr], part.keep, m_sc[g, rows]) for g in (2 * h, 2 * h + 1)]
            p = jnp.concatenate([st[2] for st in stats], axis=0)
            pv = jnp.dot(p, _with_ones(v2, h), preferred_element_type=F32)
            for n, g in enumerate((2 * h, 2 * h + 1)):
                acc_sc[g, rows] = stats[n][1] * acc_sc[g, rows] + pv[n * nr:(n + 1) * nr]
                m_sc[g, rows] = stats[n][0]

    _causal_sweep(i, scores, consume, tq, group=4, scores_first=False)

    lp = lp_ref[...]
    lam = (jnp.exp(jnp.sum(lp[0:1] * lp[1:2], axis=-1, keepdims=True))
           - jnp.exp(jnp.sum(lp[2:3] * lp[3:4], axis=-1, keepdims=True)) + lam_init)
    norm = [_normalize(acc_sc[g]) for g in range(nmap)]
    first = lane < HEAD_DIM
    o = jnp.where(first, norm[0] - lam * norm[1], norm[2] - lam * norm[3])
    osq = o * o
    ms0 = jnp.sum(jnp.where(first, osq, 0.0), axis=-1, keepdims=True) * (1.0 / HEAD_DIM)
    ms1 = jnp.sum(jnp.where(first, 0.0, osq), axis=-1, keepdims=True) * (1.0 / HEAD_DIM)
    o = o * lax.rsqrt(jnp.where(first, ms0, ms1) + SUBLN_EPS)
    o_ref[0] = (o * g_ref[...] * (1.0 - lam_init)).astype(o_ref.dtype)


def _diff_attention(qa, ka, va, lam_params, subln_g, lam_init):
    bn, t, _ = qa.shape
    tq = TQ
    g2 = jnp.concatenate([subln_g, subln_g]).reshape(1, LANES).astype(F32)
    qblk = pl.BlockSpec((1, tq, LANES), lambda b, p, i: (b, i, p))
    full = pl.BlockSpec((1, t, LANES), lambda b, p, i: (b, 0, p))
    return pl.pallas_call(
        functools.partial(_diff_kernel, tq=tq, lam_init=lam_init),
        grid=(bn, A_HEADS // 2, t // tq),
        in_specs=[qblk, full, full,
                  pl.BlockSpec((4, DIFF_QK_DIM), lambda b, p, i: (0, 0)),
                  pl.BlockSpec((1, LANES), lambda b, p, i: (0, 0))],
        out_specs=qblk,
        out_shape=jax.ShapeDtypeStruct((bn, t, A_W), BF16),
        scratch_shapes=[pltpu.VMEM((4, tq, LANES), F32)] * 2,
        compiler_params=_cparams("parallel", "parallel", "arbitrary"),
        name="diff_attention",
    )(qa, ka, va, lam_params.astype(F32), g2)


SPLIT = 4


def _class_block(cls):
    return SPLIT * (cls % SPLIT) + cls // SPLIT


def _class_runs(u, r, d):
    g = NCLS // d
    run = BAND // g
    return [(_class_block(r + d * b) * BAND + run * u, run) for b in range(g)]


def _dilated_kernel(q_ref, k_ref, v_ref, o_ref, tmp, qcm, kcm, vcm, m_sc, acc_sc, *, tile):
    i = pl.program_id(2)
    cur = i & 1
    prev = 1 - cur
    assert tile == NCLS * BAND and NCLS == SPLIT * SPLIT

    @pl.when(i == 0)
    def _():
        kcm[1] = jnp.zeros((tile, LANES), F32)
        vcm[1] = jnp.zeros((tile, LANES), F32)

    part = tile // SPLIT

    def regroup(src_ref, store):
        for c1 in range(SPLIT):
            tmp[c1 * part:(c1 + 1) * part, :] = src_ref[0, pl.ds(c1, part, stride=SPLIT), :]
        for blk in range(NCLS):
            c1, c2 = divmod(blk, SPLIT)
            store(slice(blk * BAND, (blk + 1) * BAND), tmp[pl.ds(c1 * part + c2, BAND, stride=SPLIT), :])

    def store_q(dst, val):
        qcm[dst, :] = val

    def store_k(dst, val):
        kcm[cur, dst, :] = val

    def store_v(dst, val):
        vcm[cur, dst, :] = val

    regroup(q_ref, store_q)
    regroup(k_ref, store_k)
    regroup(v_ref, store_v)

    row = lax.broadcasted_iota(jnp.int32, (BAND, LANES), 0)
    lane = lax.broadcasted_iota(jnp.int32, (BAND, LANES), 1)

    def run_rows(start, n):
        return pl.ds(pl.multiple_of(start, 8), n)

    def gather(ref, slot, runs):
        pieces = [ref[run_rows(s, n), :] if slot is None else ref[slot, run_rows(s, n), :] for s, n in runs]
        return pieces[0] if len(pieces) == 1 else jnp.concatenate(pieces, axis=0)

    bodies = []
    for pat, (window, d) in enumerate(DILATED_PATTERNS):
        assert window == BAND * d and NCLS % d == 0
        g = NCLS // d
        run = BAND // g
        nq = g * (row % run) + row // run
        nk = g * (lane % run) + lane // run
        bias_cur = jnp.where(nk <= nq, 0.0, NEG)
        bias_prev = jnp.where(nk >= nq, 0.0, NEG)
        bias_first = jnp.where(i > 0, bias_prev, NEG)

        def body(idx, d=d, g=g, pat=pat, bias_cur=bias_cur, bias_prev=bias_prev, bias_first=bias_first):
            u = idx // d
            r = idx - u * d
            first = u == 0
            runs = _class_runs(u, r, d)
            pruns = _class_runs(jnp.where(first, g - 1, u - 1), r, d)
            pslot = jnp.where(first, prev, cur)
            q = gather(qcm, None, runs).astype(BF16)
            zero = jnp.zeros_like(q)
            qcat = jnp.concatenate([jnp.where(_own_half(lane, h), q, zero) for h in range(2)], axis=0)
            kcat = jnp.concatenate([gather(kcm, pslot, pruns), gather(kcm, cur, runs)], axis=0).astype(BF16)
            vcat = jnp.concatenate([gather(vcm, pslot, pruns), gather(vcm, cur, runs)], axis=0).astype(BF16)
            bias = jnp.concatenate([jnp.where(first, bias_first, bias_prev), bias_cur], axis=1)
            s = _nt_dot(qcat, kcat)
            for h in range(2):
                sh = s[h * BAND:(h + 1) * BAND] + bias
                m = jnp.broadcast_to(jnp.max(sh, axis=-1, keepdims=True), (BAND, LANES))
                p = jnp.concatenate([jnp.exp2(sb - m).astype(BF16) for sb in _lane_blocks(sh)], axis=1)
                pv = jnp.dot(p, _with_ones(vcat, h), preferred_element_type=F32)
                off = 0
                for start, n in runs:
                    acc_sc[2 * pat + h, run_rows(start, n), :] = pv[off:off + n]
                    m_sc[2 * pat + h, run_rows(start, n), :] = m[off:off + n]
                    off += n

        bodies.append(body)

    def group(it, carry):
        for k in range(DIL_UNROLL):
            for body in bodies:
                body(it * DIL_UNROLL + k)
        return carry

    lax.fori_loop(0, NCLS // DIL_UNROLL, group, 0)

    lane_t = lax.broadcasted_iota(jnp.int32, (tile, LANES), 1)
    npat = len(DILATED_PATTERNS)
    outs = []
    for h in range(2):
        ms = [m_sc[2 * pat + h] for pat in range(npat)]
        m_all = functools.reduce(jnp.maximum, ms)
        tot = sum(jnp.exp2(ms[pat] - m_all) * acc_sc[2 * pat + h] for pat in range(npat))
        outs.append(_normalize(tot))
    res = jnp.where(lane_t < HEAD_DIM, outs[0], outs[1])
    for c in range(NCLS):
        blk = _class_block(c)
        o_ref[0, pl.ds(c, BAND, stride=NCLS), :] = res[blk * BAND:(blk + 1) * BAND]


def _dilated_attention(qb, kb, vb):
    bn, t, _ = qb.shape
    tile = DIL_TILE
    npat = len(DILATED_PATTERNS)
    blk = pl.BlockSpec((1, tile, LANES), lambda b, p, i: (b, i, p))
    return pl.pallas_call(
        functools.partial(_dilated_kernel, tile=tile),
        grid=(bn, B_HEADS // 2, t // tile),
        in_specs=[blk, blk, blk],
        out_specs=blk,
        out_shape=jax.ShapeDtypeStruct((bn, t, B_W), F32),
        scratch_shapes=[pltpu.VMEM((tile, LANES), F32)] * 2 + [pltpu.VMEM((2, tile, LANES), F32),
                        pltpu.VMEM((2, tile, LANES), F32)] + [pltpu.VMEM((2 * npat, tile, LANES), F32)] * 2,
        compiler_params=_cparams("parallel", "parallel", "arbitrary"),
        name="dilated_attention",
    )(qb, kb, vb)


FF_CHUNK = 256
HALO = 8


def _resident(shape):
    return pl.BlockSpec(shape, lambda b, i: (0,) * len(shape), pipeline_mode=pl.Buffered(1))


def _mix_up_kernel(oa_ref, ob_ref, oc_ref, x_ref, wo_ref, gate_ref, g_ref, sc_ref, sh_ref,
                   w_ref, cw_ref, cb_ref, x1_ref, y_ref, halo_sc, g_sc):
    i = pl.program_id(1)

    @pl.when(i == 0)
    def _():
        halo_sc[...] = jnp.zeros(halo_sc.shape, F32)

    o = jnp.concatenate([oa_ref[0], ob_ref[0].astype(BF16), oc_ref[0]], axis=-1)
    x1 = x_ref[0] + gate_ref[0] * jnp.dot(o, wo_ref[0], preferred_element_type=F32)
    x1_ref[0] = x1
    h = _modulated_norm(x1, g_ref[...], sc_ref[0], sh_ref[0]).astype(BF16)
    tm = h.shape[0]
    for c in range(D_FF // FF_CHUNK):
        cs = slice(c * FF_CHUNK, (c + 1) * FF_CHUNK)
        u = jnp.dot(h, w_ref[0, :, cs], preferred_element_type=F32)
        g = jnp.dot(h, w_ref[0, :, D_FF + c * FF_CHUNK:D_FF + (c + 1) * FF_CHUNK], preferred_element_type=F32)
        g_sc[0:HALO, :] = halo_sc[:, cs]
        g_sc[HALO:HALO + tm, :] = g
        halo_sc[:, cs] = g[tm - HALO:tm, :]
        conv = (cb_ref[:, cs] + g_sc[HALO - 2:HALO - 2 + tm, :] * cw_ref[0:1, cs]
                + g_sc[HALO - 1:HALO - 1 + tm, :] * cw_ref[1:2, cs] + g * cw_ref[2:3, cs])
        act = conv * (1.0 / (1.0 + jnp.exp(-conv)))
        y_ref[0, :, cs] = (act * u).astype(BF16)


def _mix_up_projection(oa, ob, oc, x, w_out, gate1, g_mlp, scale2, shift2, w_up, layer, conv_w, conv_b):
    bn, t, d = x.shape
    tok = lambda w: pl.BlockSpec((1, TM, w), lambda b, i: (b, i, 0))
    per_b = pl.BlockSpec((1, 1, d), lambda b, i: (b, 0, 0))
    once = dict(pipeline_mode=pl.Buffered(1))
    return pl.pallas_call(
        _mix_up_kernel,
        grid=(bn, t // TM),
        in_specs=[tok(A_W), tok(B_W), tok(C_W), tok(d), _layer_block(w_out, layer, **once), per_b,
                  _resident((1, d)), per_b, per_b,
                  _layer_block(w_up, layer, **once), _resident((CONV_WIDTH, D_FF)), _resident((1, D_FF))],
        out_specs=[tok(d), tok(D_FF)],
        out_shape=[jax.ShapeDtypeStruct((bn, t, d), F32), jax.ShapeDtypeStruct((bn, t, D_FF), BF16)],
        scratch_shapes=[pltpu.VMEM((HALO, D_FF), F32), pltpu.VMEM((HALO + TM, FF_CHUNK), F32)],
        compiler_params=_cparams("parallel", "arbitrary"),
        name="mix_up_projection",
    )(oa, ob, oc, x, w_out, gate1, g_mlp.reshape(1, d), scale2, shift2, w_up, conv_w, conv_b.reshape(1, D_FF))


def _down_kernel(y_ref, x_ref, w_ref, gate_ref, g_ref, o_ref, *, final_norm):
    x2 = x_ref[0] + gate_ref[0] * jnp.dot(y_ref[0], w_ref[0], preferred_element_type=F32)
    if final_norm:
        ms = jnp.mean(x2 * x2, axis=-1, keepdims=True)
        x2 = x2 * lax.rsqrt(ms + NORM_EPS) * g_ref[...]
    o_ref[0] = x2


def _down_projection(y, x1, w_down, layer, gate2, g_final, final_norm):
    bn, t, d = x1.shape
    return pl.pallas_call(
        functools.partial(_down_kernel, final_norm=final_norm),
        grid=(bn, t // TM),
        in_specs=[pl.BlockSpec((1, TM, D_FF), lambda b, i: (b, i, 0)),
                  pl.BlockSpec((1, TM, d), lambda b, i: (b, i, 0)),
                  _layer_block(w_down, layer),
                  pl.BlockSpec((1, 1, d), lambda b, i: (b, 0, 0)),
                  pl.BlockSpec((1, d), lambda b, i: (0, 0))],
        out_specs=pl.BlockSpec((1, TM, d), lambda b, i: (b, i, 0)),
        out_shape=jax.ShapeDtypeStruct((bn, t, d), F32),
        compiler_params=_cparams("parallel", "parallel"),
        name="down_projection",
    )(y, x1, w_down, gate2, g_final.reshape(1, d))


def _rope_table(t, dim):
    inv = 1.0 / (ROPE_THETA ** (jnp.arange(0, dim, 2, dtype=F32) / dim))
    ang = jnp.arange(t, dtype=F32)[:, None] * inv[None, :]
    cos, sin = jnp.cos(ang), jnp.sin(ang)
    reps = LANES // dim
    return (jnp.tile(jnp.concatenate([cos, cos], axis=-1), (1, reps)),
            jnp.tile(jnp.concatenate([-sin, sin], axis=-1), (1, reps)))


def kernel(x, c, w_mod, b_mod, g_attn, w_in, diff_lambda, subln_g, forget_bias, w_out,
           g_mlp, w_up, conv_w, conv_b, w_down, g_final):
    depth = w_mod.shape[0]
    bn, t, d = x.shape
    assert d == D_MODEL and t % DIL_TILE == 0 and t % TM == 0 and t % TQ == 0
    tabs = _rope_table(t, DIFF_QK_DIM) + _rope_table(t, HEAD_DIM)
    mod = _modulation(c, w_mod, b_mod)
    w_in_p = jnp.pad(w_in, ((0, 0), (0, 0), (0, IN_COLS_PAD - IN_COLS))).astype(BF16)
    w_out_b, w_up_b, w_down_b = w_out.astype(BF16), w_up.astype(BF16), w_down.astype(BF16)

    for layer in range(depth):
        lam_init = 0.8 - 0.6 * math.exp(-0.3 * layer)
        shift1, scale1, gate1, shift2, scale2, gate2 = [
            m[:, None, :] for m in jnp.split(mod[layer], 6, axis=-1)]
        qa, ka, va, qb, kb, vb, qc, kc, vc, fz = _in_projection(
            x, g_attn[layer], scale1, shift1, w_in_p, layer, tabs)
        q_aug, k_aug = _bias_operands(_cum_logf(fz, forget_bias[layer]))
        oa = _diff_attention(qa, ka, va, diff_lambda[layer], subln_g[layer], lam_init)
        ob = _dilated_attention(qb, kb, vb)
        oc = _fox_attention(qc, kc, vc, q_aug, k_aug)
        x1, y = _mix_up_projection(oa, ob, oc, x, w_out_b, gate1, g_mlp[layer], scale2, shift2,
                                   w_up_b, layer, conv_w[layer], conv_b[layer])
        x = _down_projection(y, x1, w_down_b, layer, gate2, g_final, layer == depth - 1)
    return x
```
